```python
import jax, jax.numpy as jnp
from jax import lax
import numpy as np

D_MODEL = 2048
BATCH = 8
SEQ = 4096
DEPTH = 4

CHUNK = 64
EPS = 1e-6
N_EVEN = (DEPTH + 1) // 2
N_ODD = DEPTH // 2

RET_HEADS = 4
RET_DK = D_MODEL // 8
RET_DV = D_MODEL // 4
RET_THETA = 10000.0
MLA_HEADS = D_MODEL // 128
MLA_NOPE = 128
MLA_ROPE = 64
MLA_V = 128
MLA_Q_RANK = D_MODEL // 4
MLA_KV_RANK = D_MODEL // 4
MLA_THETA = 10000.0
Q_BLOCK = 128
RET_QK_W = RET_HEADS * RET_DK
RET_V_W = RET_HEADS * RET_DV
EVEN_IN = 2 * RET_QK_W + 2 * RET_V_W + MLA_Q_RANK + MLA_KV_RANK + MLA_ROPE
EVEN_OUT = RET_V_W + MLA_HEADS * MLA_V
SSD_INNER = 2 * D_MODEL
SSD_HEADDIM = 64
SSD_HEADS = SSD_INNER // SSD_HEADDIM
SSD_GROUPS = 8
SSD_STATE = 128
SSD_CONV = 4
SSD_CONV_DIM = SSD_INNER + 2 * SSD_GROUPS * SSD_STATE
ODD_IN = SSD_INNER + SSD_CONV_DIM + SSD_HEADS
FFN_HIDDEN = 256 * (-(-8 * D_MODEL // (3 * 256)))
FFN_CONV = 3

kernel_name = 'hybrid_retention_mla_ssd_convffn'


def _split(t, widths):
    idx = np.cumsum(widths)[:-1].tolist()
    return jnp.split(t, idx, axis=-1)


def rmsnorm(x, g):
    xf = x.astype(jnp.float32)
    y = xf * lax.rsqrt(jnp.mean(xf * xf, axis=-1, keepdims=True) + EPS)
    return (y * g.astype(jnp.float32)).astype(x.dtype)


def modulate(h, shift, scale):
    return h * (1.0 + scale[:, None, :]) + shift[:, None, :]


def rope(t, pos, base):
    d = t.shape[-1]
    half = d // 2
    inv = base ** (-jnp.arange(half, dtype=jnp.float32) / half)
    ang = pos.astype(jnp.float32)[:, None] * inv[None, :]
    cos = jnp.cos(ang)[:, None, :]
    sin = jnp.sin(ang)[:, None, :]
    tf = t.astype(jnp.float32)
    t1 = tf[..., :half]
    t2 = tf[..., half:]
    return jnp.concatenate([t1 * cos - t2 * sin, t1 * sin + t2 * cos], axis=-1).astype(t.dtype)


def causal_dwconv(x, w, b):
    width = w.shape[0]
    y = lax.conv_general_dilated(
        x, w[:, None, :].astype(x.dtype), window_strides=(1,), padding=[(width - 1, 0)],
        dimension_numbers=('NWC', 'WIO', 'NWC'), feature_group_count=x.shape[-1])
    return y + b.astype(y.dtype)


def retention(q, k, v, log_gamma):
    bsz, s, h, dk = q.shape
    dv = v.shape[-1]
    nc = s // CHUNK

    def to_chunks(t):
        return t.reshape(bsz, nc, CHUNK, h, t.shape[-1]).transpose(1, 0, 3, 2, 4)

    idx = jnp.arange(CHUNK, dtype=jnp.float32)
    rel = idx[:, None] - idx[None, :]
    decay = jnp.where(rel >= 0, jnp.exp(log_gamma[:, None, None] * jnp.maximum(rel, 0.0)), 0.0)
    xi = jnp.exp(log_gamma[:, None] * (idx + 1.0))[:, :, None]
    zeta = jnp.exp(log_gamma[:, None] * (CHUNK - 1.0 - idx))[:, :, None]
    chunk_decay = jnp.exp(log_gamma * CHUNK)[:, None, None]

    def step(r, qkv):
        qc, kc, vc = qkv
        inner = jnp.einsum('bhld,bhsd->bhls', qc, kc) * decay
        y = jnp.einsum('bhls,bhse->bhle', inner, vc) + jnp.einsum('bhld,bhde->bhle', qc, r) * xi
        r = chunk_decay * r + jnp.einsum('bhsd,bhse->bhde', kc, vc * zeta)
        return r, y

    r0 = jnp.zeros((bsz, h, dk, dv), jnp.float32)
    _, y = lax.scan(step, r0, (to_chunks(q), to_chunks(k), to_chunks(v)))
    return y.transpose(1, 0, 3, 2, 4).reshape(bsz, s, h, dv)


def mla_attention(q_nope, q_rope, k_nope, k_rope, v):
    bsz, s, h, _ = q_nope.shape
    nb = s // Q_BLOCK
    scale = (MLA_NOPE + MLA_ROPE) ** -0.5
    key_chunk = jnp.arange(s) // CHUNK
    qn_b = q_nope.reshape(bsz, nb, Q_BLOCK, h, MLA_NOPE).transpose(1, 0, 2, 3, 4)
    qr_b = q_rope.reshape(bsz, nb, Q_BLOCK, h, MLA_ROPE).transpose(1, 0, 2, 3, 4)

    def one_block(args):
        i, qn, qr = args
        sc = (jnp.einsum('bqhd,bkhd->bhqk', qn, k_nope).astype(jnp.float32)
              + jnp.einsum('bqhr,bkr->bhqk', qr, k_rope).astype(jnp.float32)) * scale
        q_chunk = (i * Q_BLOCK + jnp.arange(Q_BLOCK)) // CHUNK
        mask = key_chunk[None, :] <= q_chunk[:, None]
        sc = jnp.where(mask, sc, -jnp.inf)
        p = jax.nn.softmax(sc, axis=-1).astype(v.dtype)
        return jnp.einsum('bhqk,bkhd->bqhd', p, v)

    out = lax.map(one_block, (jnp.arange(nb), qn_b, qr_b))
    return out.transpose(1, 0, 2, 3, 4).reshape(bsz, s, h, v.shape[-1])


def hybrid_mixer(h, pos, w_in, q_norm_g, w_uq, kv_norm_g, w_ukv, ret_gn_g, w_out):
    bsz, s, _ = h.shape
    rq, rk, rv, rg, cq, ckv, kr = _split(
        h @ w_in, [RET_QK_W, RET_QK_W, RET_V_W, RET_V_W, MLA_Q_RANK, MLA_KV_RANK, MLA_ROPE])
    log_gamma = jnp.log1p(-jnp.exp2(-5.0 - jnp.arange(RET_HEADS, dtype=jnp.float32)))
    rq = rope(rq.reshape(bsz, s, RET_HEADS, RET_DK), pos, RET_THETA)
    rk = rope(rk.reshape(bsz, s, RET_HEADS, RET_DK), pos, RET_THETA) * (RET_DK ** -0.5)
    rv = rv.reshape(bsz, s, RET_HEADS, RET_DV)
    yr = retention(rq.astype(jnp.float32), rk.astype(jnp.float32), rv.astype(jnp.float32), log_gamma)
    yc = yr - jnp.mean(yr, axis=-1, keepdims=True)
    yr = yc * lax.rsqrt(jnp.mean(yc * yc, axis=-1, keepdims=True) + EPS)
    yr = yr.reshape(bsz, s, RET_V_W) * ret_gn_g.astype(jnp.float32)
    y_ret = (jax.nn.silu(rg.astype(jnp.float32)) * yr).astype(h.dtype)
    q = (rmsnorm(cq, q_norm_g) @ w_uq).reshape(bsz, s, MLA_HEADS, MLA_NOPE + MLA_ROPE)
    q_nope = q[..., :MLA_NOPE]
    q_rope = rope(q[..., MLA_NOPE:], pos, MLA_THETA)
    kv = (rmsnorm(ckv, kv_norm_g) @ w_ukv).reshape(bsz, s, MLA_HEADS, MLA_NOPE + MLA_V)
    k_nope = kv[..., :MLA_NOPE]
    v = kv[..., MLA_NOPE:]
    k_rope = rope(kr[:, :, None, :], pos, MLA_THETA)[:, :, 0, :]
    y_mla = mla_attention(q_nope, q_rope, k_nope, k_rope, v).reshape(bsz, s, MLA_HEADS * MLA_V)
    return jnp.concatenate([y_ret, y_mla], axis=-1) @ w_out


def ssd_scan(x, a, bm, cm):
    bsz, s, h, p = x.shape
    g, n = bm.shape[-2:]
    j = h // g
    nc = s // CHUNK
    xc = x.reshape(bsz, nc, CHUNK, g, j, p).transpose(1, 0, 2, 3, 4, 5)
    ac = a.reshape(bsz, nc, CHUNK, g, j).transpose(1, 0, 3, 4, 2)
    bc = bm.reshape(bsz, nc, CHUNK, g, n).transpose(1, 0, 2, 3, 4)
    cc = cm.reshape(bsz, nc, CHUNK, g, n).transpose(1, 0, 2, 3, 4)
    causal = jnp.tril(jnp.ones((CHUNK, CHUNK), dtype=bool))

    def step(state, inp):
        xk, ak, bk, ck = inp
        acum = jnp.cumsum(ak, axis=-1)
        seg = acum[..., :, None] - acum[..., None, :]
        lmat = jnp.where(causal, jnp.exp(jnp.where(causal, seg, 0.0)), 0.0)
        cb = jnp.einsum('blgn,bsgn->bgls', ck, bk)
        y_diag = jnp.einsum('bgls,bgjls,bsgjp->blgjp', cb, lmat, xk)
        y_off = jnp.einsum('blgn,bgjpn,bgjl->blgjp', ck, state, jnp.exp(acum))
        to_end = jnp.exp(acum[..., -1:] - acum)
        state = (state * jnp.exp(acum[..., -1])[..., None, None]
                 + jnp.einsum('bsgn,bgjs,bsgjp->bgjpn', bk, to_end, xk))
        return state, y_diag + y_off

    s0 = jnp.zeros((bsz, g, j, p, n), jnp.float32)
    _, y = lax.scan(step, s0, (xc, ac, bc, cc))
    return y.transpose(1, 0, 2, 3, 4, 5).reshape(bsz, s, h, p)


def ssd_mixer(h, w_in, conv_w, conv_b, dt_bias, a_log, d_skip, norm_g, w_out):
    bsz, s, _ = h.shape
    z, xbc, dt = _split(h @ w_in, [SSD_INNER, SSD_CONV_DIM, SSD_HEADS])
    xbc = jax.nn.silu(causal_dwconv(xbc, conv_w, conv_b))
    xs, bm, cm = _split(xbc, [SSD_INNER, SSD_GROUPS * SSD_STATE, SSD_GROUPS * SSD_STATE])
    xs = xs.reshape(bsz, s, SSD_HEADS, SSD_HEADDIM).astype(jnp.float32)
    bm = bm.reshape(bsz, s, SSD_GROUPS, SSD_STATE).astype(jnp.float32)
    cm = cm.reshape(bsz, s, SSD_GROUPS, SSD_STATE).astype(jnp.float32)
    dt = jax.nn.softplus(dt.astype(jnp.float32) + dt_bias.astype(jnp.float32))
    a = -jnp.exp(a_log.astype(jnp.float32))
    y = ssd_scan(xs * dt[..., None], dt * a, bm, cm)
    y = y + xs * d_skip.astype(jnp.float32)[:, None]
    y = y.reshape(bsz, s, SSD_INNER) * jax.nn.silu(z.astype(jnp.float32))
    return rmsnorm(y, norm_g).astype(h.dtype) @ w_out


def conv_ffn(h, w_up, conv_w, conv_b, w_down):
    u, g = jnp.split(h @ w_up, 2, axis=-1)
    g = causal_dwconv(g, conv_w, conv_b)
    return (jax.nn.silu(g) * u) @ w_down


def setup_inputs(seed: int = 0) -> dict:
    key = jax.random.key(seed)
    ks = jax.random.split(key, 32)
    f32 = jnp.float32

    def nrm(k, shape, fan_in, scale=1.0):
        return jax.random.normal(k, shape, f32) * (scale * fan_in ** -0.5)

    def gain(k, shape):
        return 1.0 + 0.02 * jax.random.normal(k, shape, f32)

    def small(k, shape):
        return 0.02 * jax.random.normal(k, shape, f32)

    dt0 = jnp.exp(jax.random.uniform(ks[17], (N_ODD, SSD_HEADS), f32, jnp.log(1e-3), jnp.log(1e-1)))
    return {
        'x': jax.random.normal(ks[0], (BATCH, SEQ, D_MODEL), f32),
        'c': jax.random.normal(ks[1], (BATCH, D_MODEL), f32),
        'ada_w': nrm(ks[2], (DEPTH, D_MODEL, 6 * D_MODEL), D_MODEL, 0.5),
        'ada_b': small(ks[3], (DEPTH, 6 * D_MODEL)),
        'norm_mix_g': gain(ks[4], (DEPTH, D_MODEL)),
        'norm_ffn_g': gain(ks[5], (DEPTH, D_MODEL)),
        'hyb_w_in': nrm(ks[6], (N_EVEN, D_MODEL, EVEN_IN), D_MODEL),
        'hyb_q_norm_g': gain(ks[7], (N_EVEN, MLA_Q_RANK)),
        'hyb_w_uq': nrm(ks[8], (N_EVEN, MLA_Q_RANK, MLA_HEADS * (MLA_NOPE + MLA_ROPE)), MLA_Q_RANK),
        'hyb_kv_norm_g': gain(ks[9], (N_EVEN, MLA_KV_RANK)),
        'hyb_w_ukv': nrm(ks[10], (N_EVEN, MLA_KV_RANK, MLA_HEADS * (MLA_NOPE + MLA_V)), MLA_KV_RANK),
        'hyb_ret_gn_g': gain(ks[11], (N_EVEN, RET_V_W)),
        'hyb_w_out': nrm(ks[12], (N_EVEN, EVEN_OUT, D_MODEL), EVEN_OUT),
        'ssd_w_in': nrm(ks[13], (N_ODD, D_MODEL, ODD_IN), D_MODEL),
        'ssd_conv_w': nrm(ks[14], (N_ODD, SSD_CONV, SSD_CONV_DIM), SSD_CONV),
        'ssd_conv_b': small(ks[15], (N_ODD, SSD_CONV_DIM)),
        'ssd_dt_bias': dt0 + jnp.log(-jnp.expm1(-dt0)),
        'ssd_a_log': jnp.log(jax.random.uniform(ks[18], (N_ODD, SSD_HEADS), f32, 1.0, 16.0)),
        'ssd_d': gain(ks[19], (N_ODD, SSD_HEADS)),
        'ssd_norm_g': gain(ks[20], (N_ODD, SSD_INNER)),
        'ssd_w_out': nrm(ks[21], (N_ODD, SSD_INNER, D_MODEL), SSD_INNER),
        'ffn_w_up': nrm(ks[22], (DEPTH, D_MODEL, 2 * FFN_HIDDEN), D_MODEL),
        'ffn_conv_w': nrm(ks[23], (DEPTH, FFN_CONV, FFN_HIDDEN), FFN_CONV),
        'ffn_conv_b': small(ks[24], (DEPTH, FFN_HIDDEN)),
        'ffn_w_down': nrm(ks[25], (DEPTH, FFN_HIDDEN, D_MODEL), FFN_HIDDEN),
        'final_norm_g': gain(ks[26], (D_MODEL,)),
    }


def reference(x, c, ada_w, ada_b, norm_mix_g, norm_ffn_g,
              hyb_w_in, hyb_q_norm_g, hyb_w_uq, hyb_kv_norm_g, hyb_w_ukv, hyb_ret_gn_g, hyb_w_out,
              ssd_w_in, ssd_conv_w, ssd_conv_b, ssd_dt_bias, ssd_a_log, ssd_d, ssd_norm_g, ssd_w_out,
              ffn_w_up, ffn_conv_w, ffn_conv_b, ffn_w_down, final_norm_g):
    pos = jnp.arange(x.shape[1], dtype=jnp.int32)
    mods = jnp.einsum('bd,lde->lbe', jax.nn.silu(c), ada_w) + ada_b[:, None, :]
    for l in range(DEPTH):
        sh_m, sc_m, gt_m, sh_f, sc_f, gt_f = jnp.split(mods[l], 6, axis=-1)
        hm = modulate(rmsnorm(x, norm_mix_g[l]), sh_m, sc_m)
        i = l // 2
        if l % 2 == 0:
            y = hybrid_mixer(hm, pos, hyb_w_in[i], hyb_q_norm_g[i], hyb_w_uq[i], hyb_kv_norm_g[i],
                             hyb_w_ukv[i], hyb_ret_gn_g[i], hyb_w_out[i])
        else:
            y = ssd_mixer(hm, ssd_w_in[i], ssd_conv_w[i], ssd_conv_b[i], ssd_dt_bias[i], ssd_a_log[i],
                          ssd_d[i], ssd_norm_g[i], ssd_w_out[i])
        x = x + gt_m[:, None, :] * y
        hf = modulate(rmsnorm(x, norm_ffn_g[l]), sh_f, sc_f)
        x = x + gt_f[:, None, :] * conv_ffn(hf, ffn_w_up[l], ffn_conv_w[l], ffn_conv_b[l], ffn_w_down[l])
    return rmsnorm(x, final_norm_g)
```

```python
import functools
import math

import numpy as np
import jax
import jax.numpy as jnp
from jax import lax
from jax.experimental import pallas as pl
from jax.experimental.pallas import tpu as pltpu

F32 = jnp.float32
BF16 = jnp.bfloat16

D_MODEL = 2048
DEPTH = 4
EPS = 1e-6

RET_HEADS = 4
RET_DK = 256
RET_DV = 512
RET_THETA = 10000.0
RET_QK_W = RET_HEADS * RET_DK
RET_V_W = RET_HEADS * RET_DV

MLA_HEADS = 16
MLA_NOPE = 128
MLA_ROPE = 64
MLA_V = 128
MLA_RANK = 512
MLA_THETA = 10000.0
MLA_QK_PAD = 256
MASK_CHUNK = 64

SSD_INNER = 4096
SSD_HEADDIM = 64
SSD_HEADS = 64
SSD_GROUPS = 8
SSD_HPG = SSD_HEADS // SSD_GROUPS
SSD_STATE = 128
SSD_CONV = 4
SSD_CONV_DIM = SSD_INNER + 2 * SSD_GROUPS * SSD_STATE
SSD_GW = SSD_HPG * SSD_HEADDIM

FFN_HIDDEN = 5632
FFN_CONV = 3

LANE = 128
SUBLANE = 8
VMEM_CAP = 56 * 1024 * 1024
NEG = -1e30

TM_PROJ = 1024
TN_PROJ = 1024
TH_FFN = 512
TN_RES = 512
L_RET = 256
L_SSD = 256
TQ_ATTN = 256
TM_UP = 512
TM_CONV = 512


def _pick(n, pref):
    t = pref
    while t > 1 and n % t:
        t //= 2
    return t if n % t == 0 else n


def _params(n_axes, vmem_bytes):
    limit = int(min(VMEM_CAP, max(16 * 1024 * 1024, vmem_bytes * 1.3 + (4 << 20))))
    return pltpu.CompilerParams(dimension_semantics=("arbitrary",) * n_axes,
                                vmem_limit_bytes=limit)


def _silu(v):
    return v * (1.0 / (1.0 + jnp.exp(-v)))


def _dot(a, b):
    return jnp.dot(a, b, preferred_element_type=F32)


def _dot_nt(a, b):
    return lax.dot_general(a, b, (((1,), (1,)), ((), ())), preferred_element_type=F32)


def _shift_rows(g, k, prev8):
    c = g.shape[1]
    r = pltpu.roll(g, k, 0)
    hr = pltpu.roll(prev8, k, 0)
    rows = lax.broadcasted_iota(jnp.int32, (SUBLANE, c), 0)
    top = jnp.where(rows < k, hr, r[0:SUBLANE])
    return jnp.concatenate([top, r[SUBLANE:]], axis=0)


def _causal_conv(g, w_ref, b_ref, prev8):
    width = w_ref.shape[0]
    y = g * w_ref[width - 1:width, :] + b_ref[...]
    for k in range(1, width):
        y = y + _shift_rows(g, k, prev8) * w_ref[width - 1 - k:width - k, :]
    return y


def _mods_kernel(c_ref, w_ref, b_ref, o_ref):
    h = _silu(c_ref[...]).astype(BF16)
    o_ref[...] = _dot(h, w_ref[...].astype(BF16)) + b_ref[...]


def _mods(c, ada_w, ada_b):
    depth, d, n = ada_w.shape
    b = c.shape[0]
    tn = _pick(n, 1024)
    return pl.pallas_call(
        _mods_kernel,
        grid=(depth, n // tn),
        in_specs=[pl.BlockSpec((b, d), lambda l, j: (0, 0)),
                  pl.BlockSpec((None, d, tn), lambda l, j: (l, 0, j)),
                  pl.BlockSpec((None, 1, tn), lambda l, j: (l, 0, j))],
        out_specs=pl.BlockSpec((None, b, tn), lambda l, j: (l, 0, j)),
        out_shape=jax.ShapeDtypeStruct((depth, b, n), F32),
        compiler_params=_params(2, 2 * d * tn * 4 + d * tn * 2),
        name="adaln_mods",
    )(c, ada_w, ada_b.reshape(depth, 1, n))


def _norm_mod_rows(x_ref, g_ref, sh_ref, sc_ref, h_scr):
    tm = x_ref.shape[0]
    rc = _pick(tm, 256)
    g = g_ref[...]
    sc1 = 1.0 + sc_ref[...]
    sh = sh_ref[...]

    def body(r, carry):
        rows = pl.ds(pl.multiple_of(r * rc, rc), rc)
        xf = x_ref[rows, :]
        inv = lax.rsqrt(jnp.mean(xf * xf, axis=-1, keepdims=True) + EPS)
        h_scr[rows, :] = ((xf * inv * g) * sc1 + sh).astype(BF16)
        return carry

    lax.fori_loop(0, tm // rc, body, 0)


def _proj_kernel(x_ref, g_ref, sh_ref, sc_ref, w_ref, o_ref, h_scr):
    @pl.when(pl.program_id(2) == 0)
    def _():
        _norm_mod_rows(x_ref, g_ref, sh_ref, sc_ref, h_scr)

    o_ref[...] = _dot(h_scr[...], w_ref[...]).astype(o_ref.dtype)


def _norm_mod_matmul(x, g, shift, scale, w, out_dtype, name):
    b, s, d = x.shape
    n = w.shape[1]
    tm = _pick(s, TM_PROJ)
    tn = _pick(n, TN_PROJ) if n % TN_PROJ == 0 else n
    osz = jnp.dtype(out_dtype).itemsize
    vmem = 2 * tm * d * 4 + tm * d * 2 + 2 * d * tn * 2 + 2 * tm * tn * osz + tm * tn * 4
    return pl.pallas_call(
        _proj_kernel,
        grid=(b, s // tm, n // tn),
        in_specs=[pl.BlockSpec((None, tm, d), lambda bi, i, j: (bi, i, 0)),
                  pl.BlockSpec((1, d), lambda bi, i, j: (0, 0)),
                  pl.BlockSpec((None, 1, d), lambda bi, i, j: (bi, 0, 0)),
                  pl.BlockSpec((None, 1, d), lambda bi, i, j: (bi, 0, 0)),
                  pl.BlockSpec((d, tn), lambda bi, i, j: (0, j))],
        out_specs=pl.BlockSpec((None, tm, tn), lambda bi, i, j: (bi, i, j)),
        out_shape=jax.ShapeDtypeStruct((b, s, n), out_dtype),
        scratch_shapes=[pltpu.VMEM((tm, d), BF16)],
        compiler_params=_params(3, vmem),
        name=name,
    )(x, g.reshape(1, d), shift, scale, w)


def _ffn_up_kernel(x_ref, g_ref, sh_ref, sc_ref, wu_ref, wg_ref, cw_ref, cb_ref, o_ref,
                   h_scr, halo_scr):
    si = pl.program_id(1)
    j = pl.program_id(2)

    @pl.when(j == 0)
    def _():
        _norm_mod_rows(x_ref, g_ref, sh_ref, sc_ref, h_scr)

    h = h_scr[...]
    u = _dot(h, wu_ref[...])
    gate = _dot(h, wg_ref[...])
    tm = gate.shape[0]
    prev8 = jnp.where(si == 0, 0.0, halo_scr[j])
    halo_scr[j] = gate[tm - SUBLANE:tm, :]
    gc = _causal_conv(gate, cw_ref, cb_ref, prev8)
    o_ref[...] = (_silu(gc) * u).astype(o_ref.dtype)


def _ffn_up(x, g, shift, scale, w_up, conv_w, conv_b):
    b, s, d = x.shape
    hid = conv_w.shape[1]
    tm = _pick(s, TM_PROJ)
    th = _pick(hid, TH_FFN)
    nj = hid // th
    vmem = (2 * tm * d * 4 + tm * d * 2 + 4 * d * th * 2 + 2 * tm * th * 2
            + 6 * tm * th * 4)
    return pl.pallas_call(
        _ffn_up_kernel,
        grid=(b, s // tm, nj),
        in_specs=[pl.BlockSpec((None, tm, d), lambda bi, i, j: (bi, i, 0)),
                  pl.BlockSpec((1, d), lambda bi, i, j: (0, 0)),
                  pl.BlockSpec((None, 1, d), lambda bi, i, j: (bi, 0, 0)),
                  pl.BlockSpec((None, 1, d), lambda bi, i, j: (bi, 0, 0)),
                  pl.BlockSpec((d, th), lambda bi, i, j: (0, j)),
                  pl.BlockSpec((d, th), lambda bi, i, j: (0, j + nj)),
                  pl.BlockSpec((FFN_CONV, th), lambda bi, i, j: (0, j)),
                  pl.BlockSpec((1, th), lambda bi, i, j: (0, j))],
        out_specs=pl.BlockSpec((None, tm, th), lambda bi, i, j: (bi, i, j)),
        out_shape=jax.ShapeDtypeStruct((b, s, hid), BF16),
        scratch_shapes=[pltpu.VMEM((tm, d), BF16),
                        pltpu.VMEM((nj, SUBLANE, th), F32)],
        compiler_params=_params(3, vmem),
        name="ffn_up",
    )(x, g.reshape(1, d), shift, scale, w_up, w_up, conv_w, conv_b.reshape(1, hid))


def _matmul_res_kernel(*refs, n_a):
    a_refs = refs[:n_a]
    w_refs = refs[n_a:2 * n_a]
    x_ref, gt_ref, o_ref = refs[2 * n_a:]
    acc = _dot(a_refs[0][...], w_refs[0][...])
    for a_ref, w_ref in zip(a_refs[1:], w_refs[1:]):
        acc = acc + _dot(a_ref[...], w_ref[...])
    o_ref[...] = x_ref[...] + gt_ref[...] * acc


def _matmul_res(a_list, w, x, gate, name):
    b, s, n = x.shape
    ks = [a.shape[2] for a in a_list]
    tm = _pick(s, TM_PROJ)
    tn = _pick(n, TN_RES)
    in_specs = [pl.BlockSpec((None, tm, k), lambda bi, i, j: (bi, i, 0)) for k in ks]
    assert len(set(ks)) == 1
    in_specs += [pl.BlockSpec((ks[0], tn), functools.partial(lambda bi, i, j, p: (p, j), p=p))
                 for p in range(len(ks))]
    in_specs += [pl.BlockSpec((None, tm, tn), lambda bi, i, j: (bi, i, j)),
                 pl.BlockSpec((None, 1, tn), lambda bi, i, j: (bi, 0, j))]
    ktot = sum(ks)
    vmem = 2 * tm * ktot * 2 + 2 * ktot * tn * 2 + 5 * tm * tn * 4
    return pl.pallas_call(
        functools.partial(_matmul_res_kernel, n_a=len(a_list)),
        grid=(b, s // tm, n // tn),
        in_specs=in_specs,
        out_specs=pl.BlockSpec((None, tm, tn), lambda bi, i, j: (bi, i, j)),
        out_shape=jax.ShapeDtypeStruct((b, s, n), F32),
        compiler_params=_params(3, vmem),
        name=name,
    )(*a_list, *([w] * len(a_list)), x, gate)


def _ret_log_gamma(h):
    return math.log1p(-(2.0 ** (-5.0 - h)))


def _retention_kernel(q_ref, k_ref, v_ref, g_ref, cos_ref, sin_ref, gn_ref, o_ref, r_scr):
    ci = pl.program_id(1)
    L = q_ref.shape[0]

    @pl.when(ci == 0)
    def _():
        r_scr[...] = jnp.zeros_like(r_scr)

    cos = cos_ref[...]
    sin = sin_ref[...]
    li = lax.broadcasted_iota(jnp.int32, (L, L), 0)
    si = lax.broadcasted_iota(jnp.int32, (L, L), 1)
    rel = (li - si).astype(F32)
    causal = li >= si
    pos = lax.broadcasted_iota(jnp.int32, (L, 1), 0).astype(F32)
    half = RET_DK // 2

    def rope(t):
        t1 = t[:, :half]
        t2 = t[:, half:]
        return jnp.concatenate([t1 * cos - t2 * sin, t1 * sin + t2 * cos], axis=1)

    for h in range(RET_HEADS):
        lg = _ret_log_gamma(h)
        q = rope(q_ref[:, h * RET_DK:(h + 1) * RET_DK])
        k = rope(k_ref[:, h * RET_DK:(h + 1) * RET_DK]) * (RET_DK ** -0.5)
        v = v_ref[:, h * RET_DV:(h + 1) * RET_DV].astype(F32)
        qb = q.astype(BF16)
        decay = jnp.where(causal, jnp.exp(jnp.where(causal, rel, 0.0) * lg), 0.0)
        inner = _dot_nt(qb, k.astype(BF16)) * decay
        xi = jnp.exp(lg * (pos + 1.0))
        zeta = jnp.exp(lg * (L - 1.0 - pos))
        r_old = r_scr[h]
        y = _dot(inner.astype(BF16), v.astype(BF16)) + _dot(qb, r_old.astype(BF16)) * xi
        kt = jnp.transpose(k).astype(BF16)
        r_scr[h] = math.exp(lg * L) * r_old + _dot(kt, (v * zeta).astype(BF16))
        yc = y - jnp.mean(y, axis=-1, keepdims=True)
        yn = yc * lax.rsqrt(jnp.mean(yc * yc, axis=-1, keepdims=True) + EPS)
        cols = slice(h * RET_DV, (h + 1) * RET_DV)
        gate = _silu(g_ref[:, cols].astype(F32))
        o_ref[:, cols] = (gate * (yn * gn_ref[:, cols])).astype(o_ref.dtype)


def _retention(qk, vg, cos, sin, gn_g):
    b, s, _ = qk.shape
    L = _pick(s, L_RET)
    vmem = 2 * (2 * L * RET_QK_W * 4 + 2 * L * RET_V_W * 2 + L * RET_V_W * 2) \
        + RET_HEADS * RET_DK * RET_DV * 4 + 24 * L * RET_DV * 4
    return pl.pallas_call(
        _retention_kernel,
        grid=(b, s // L),
        in_specs=[pl.BlockSpec((None, L, RET_QK_W), lambda bi, c: (bi, c, 0)),
                  pl.BlockSpec((None, L, RET_QK_W), lambda bi, c: (bi, c, 1)),
                  pl.BlockSpec((None, L, RET_V_W), lambda bi, c: (bi, c, 0)),
                  pl.BlockSpec((None, L, RET_V_W), lambda bi, c: (bi, c, 1)),
                  pl.BlockSpec((L, RET_DK // 2), lambda bi, c: (c, 0)),
                  pl.BlockSpec((L, RET_DK // 2), lambda bi, c: (c, 0)),
                  pl.BlockSpec((1, RET_V_W), lambda bi, c: (0, 0))],
        out_specs=pl.BlockSpec((None, L, RET_V_W), lambda bi, c: (bi, c, 0)),
        out_shape=jax.ShapeDtypeStruct((b, s, RET_V_W), BF16),
        scratch_shapes=[pltpu.VMEM((RET_HEADS, RET_DK, RET_DV), F32)],
        compiler_params=_params(2, vmem),
        name="retention",
    )(qk, qk, vg, vg, cos, sin, gn_g.reshape(1, RET_V_W))


def _rms_rows(x, g):
    return x * lax.rsqrt(jnp.mean(x * x, axis=-1, keepdims=True) + EPS) * g


def _rope_slot(t, cos_t, sin_t):
    lane = lax.broadcasted_iota(jnp.int32, t.shape, 1)
    hr = MLA_ROPE // 2
    swapped = jnp.where(lane < hr, pltpu.roll(t, LANE - hr, 1), pltpu.roll(t, hr, 1))
    return t * cos_t + swapped * sin_t


def _q_up_kernel(c_ref, g_ref, w_ref, cos_ref, sin_ref, o_ref):
    h = _rms_rows(c_ref[...], g_ref[...]).astype(BF16)
    q = _dot(h, w_ref[...])
    cos_t = cos_ref[...]
    sin_t = sin_ref[...]
    for hd in range(MLA_HEADS):
        base = hd * MLA_QK_PAD
        o_ref[:, base:base + MLA_NOPE] = q[:, base:base + MLA_NOPE].astype(o_ref.dtype)
        rs = q[:, base + MLA_NOPE:base + MLA_QK_PAD]
        o_ref[:, base + MLA_NOPE:base + MLA_QK_PAD] = _rope_slot(rs, cos_t, sin_t).astype(o_ref.dtype)


def _kv_up_kernel(c_ref, kr_ref, g_ref, wk_ref, wv_ref, cos_ref, sin_ref, k_ref, v_ref):
    h = _rms_rows(c_ref[...], g_ref[...]).astype(BF16)
    kn = _dot(h, wk_ref[...])
    v_ref[...] = _dot(h, wv_ref[...]).astype(v_ref.dtype)
    kr = _rope_slot(kr_ref[...], cos_ref[...], sin_ref[...]).astype(k_ref.dtype)
    for hd in range(MLA_HEADS):
        base = hd * MLA_QK_PAD
        k_ref[:, base:base + MLA_NOPE] = kn[:, hd * MLA_NOPE:(hd + 1) * MLA_NOPE].astype(k_ref.dtype)
        k_ref[:, base + MLA_NOPE:base + MLA_QK_PAD] = kr


def _mla_up(small, q_g, kv_g, wq, wk, wv, cos_t, sin_t):
    b, s, _ = small.shape
    tm = _pick(s, TM_UP)
    nq = MLA_HEADS * MLA_QK_PAD
    nv = MLA_HEADS * MLA_V
    tab = pl.BlockSpec((tm, LANE), lambda bi, i: (i, 0))
    q = pl.pallas_call(
        _q_up_kernel,
        grid=(b, s // tm),
        in_specs=[pl.BlockSpec((None, tm, MLA_RANK), lambda bi, i: (bi, i, 0)),
                  pl.BlockSpec((1, MLA_RANK), lambda bi, i: (0, 0)),
                  pl.BlockSpec((MLA_RANK, nq), lambda bi, i: (0, 0)),
                  tab, tab],
        out_specs=pl.BlockSpec((None, tm, nq), lambda bi, i: (bi, i, 0)),
        out_shape=jax.ShapeDtypeStruct((b, s, nq), BF16),
        compiler_params=_params(2, 2 * MLA_RANK * nq * 2 + 2 * tm * nq * 2 + 3 * tm * nq * 4),
        name="mla_q_up",
    )(small, q_g.reshape(1, MLA_RANK), wq, cos_t, sin_t)
    k, v = pl.pallas_call(
        _kv_up_kernel,
        grid=(b, s // tm),
        in_specs=[pl.BlockSpec((None, tm, MLA_RANK), lambda bi, i: (bi, i, 1)),
                  pl.BlockSpec((None, tm, LANE), lambda bi, i: (bi, i, 2 * MLA_RANK // LANE)),
                  pl.BlockSpec((1, MLA_RANK), lambda bi, i: (0, 0)),
                  pl.BlockSpec((MLA_RANK, nv), lambda bi, i: (0, 0)),
                  pl.BlockSpec((MLA_RANK, nv), lambda bi, i: (0, 0)),
                  tab, tab],
        out_specs=[pl.BlockSpec((None, tm, nq), lambda bi, i: (bi, i, 0)),
                   pl.BlockSpec((None, tm, nv), lambda bi, i: (bi, i, 0))],
        out_shape=[jax.ShapeDtypeStruct((b, s, nq), BF16),
                   jax.ShapeDtypeStruct((b, s, nv), BF16)],
        compiler_params=_params(2, 4 * MLA_RANK * nv * 2 + 2 * tm * (nq + nv) * 2 + 3 * tm * nq * 4),
        name="mla_kv_up",
    )(small, small, kv_g.reshape(1, MLA_RANK), wk, wv, cos_t, sin_t)
    return q, k, v


def _mla_attn_kernel(q_ref, k_ref, v_ref, o_ref):
    qi = pl.program_id(2)
    tq = q_ref.shape[0]
    q = q_ref[...]
    c = ((MLA_NOPE + MLA_ROPE) ** -0.5) * math.log2(math.e)

    def update(carry, s2, v):
        m, l, acc = carry
        m_new = jnp.maximum(m, jnp.max(s2, axis=-1, keepdims=True))
        p = jnp.exp2(s2 - m_new)
        alpha = jnp.exp2(m - m_new)
        l = alpha * l + jnp.sum(p, axis=-1, keepdims=True)
        acc = alpha * acc + _dot(p.astype(BF16), v)
        return m_new, l, acc

    def body(ki, carry):
        rows = pl.ds(pl.multiple_of(ki * tq, tq), tq)
        s2 = _dot_nt(q, k_ref[rows, :]) * c
        return update(carry, s2, v_ref[rows, :])

    init = (jnp.full((tq, 1), NEG, F32), jnp.zeros((tq, 1), F32), jnp.zeros((tq, MLA_V), F32))
    carry = lax.fori_loop(0, qi, body, init)

    rows = pl.ds(pl.multiple_of(qi * tq, tq), tq)
    s2 = _dot_nt(q, k_ref[rows, :]) * c
    qc = lax.broadcasted_iota(jnp.int32, (tq, tq), 0) // MASK_CHUNK
    kc = lax.broadcasted_iota(jnp.int32, (tq, tq), 1) // MASK_CHUNK
    s2 = jnp.where(kc <= qc, s2, NEG)
    m, l, acc = update(carry, s2, v_ref[rows, :])
    o_ref[...] = (acc / l).astype(o_ref.dtype)


def _mla_attention(q, k, v):
    b, s, _ = q.shape
    tq = _pick(s, TQ_ATTN)
    vmem = 2 * (tq * MLA_QK_PAD * 2 + s * MLA_QK_PAD * 2 + s * MLA_V * 2 + tq * MLA_V * 2) \
        + 12 * tq * tq * 4
    return pl.pallas_call(
        _mla_attn_kernel,
        grid=(b, MLA_HEADS, s // tq),
        in_specs=[pl.BlockSpec((None, tq, MLA_QK_PAD), lambda bi, h, i: (bi, i, h)),
                  pl.BlockSpec((None, s, MLA_QK_PAD), lambda bi, h, i: (bi, 0, h)),
                  pl.BlockSpec((None, s, MLA_V), lambda bi, h, i: (bi, 0, h))],
        out_specs=pl.BlockSpec((None, tq, MLA_V), lambda bi, h, i: (bi, i, h)),
        out_shape=jax.ShapeDtypeStruct((b, s, MLA_HEADS * MLA_V), BF16),
        compiler_params=_params(3, vmem),
        name="mla_attention",
    )(q, k, v)


def _ssd_prep_kernel(x_ref, cw_ref, cb_ref, o_ref, halo_scr):
    si = pl.program_id(1)
    j = pl.program_id(2)
    x = x_ref[...]
    tm = x.shape[0]
    prev8 = jnp.where(si == 0, 0.0, halo_scr[j])
    halo_scr[j] = x[tm - SUBLANE:tm, :]
    o_ref[...] = _silu(_causal_conv(x, cw_ref, cb_ref, prev8))


def _ssd_prep(xbc, conv_w, conv_b):
    b, s, n = xbc.shape
    tm = _pick(s, TM_CONV)
    tc = _pick(n, 2048)
    nj = n // tc
    return pl.pallas_call(
        _ssd_prep_kernel,
        grid=(b, s // tm, nj),
        in_specs=[pl.BlockSpec((None, tm, tc), lambda bi, i, j: (bi, i, j)),
                  pl.BlockSpec((SSD_CONV, tc), lambda bi, i, j: (0, j)),
                  pl.BlockSpec((1, tc), lambda bi, i, j: (0, j))],
        out_specs=pl.BlockSpec((None, tm, tc), lambda bi, i, j: (bi, i, j)),
        out_shape=jax.ShapeDtypeStruct((b, s, n), F32),
        scratch_shapes=[pltpu.VMEM((nj, SUBLANE, tc), F32)],
        compiler_params=_params(3, 8 * tm * tc * 4),
        name="ssd_conv",
    )(xbc, conv_w, conv_b.reshape(1, n))


def _softplus_kernel(x_ref, b_ref, o_ref):
    v = x_ref[...] + b_ref[...]
    o_ref[...] = jnp.maximum(v, 0.0) + jnp.log1p(jnp.exp(-jnp.abs(v)))


def _ssd_dt(dt_raw, dt_bias_pad):
    b, s, n = dt_raw.shape
    tm = _pick(s, 1024)
    return pl.pallas_call(
        _softplus_kernel,
        grid=(b, s // tm),
        in_specs=[pl.BlockSpec((None, tm, n), lambda bi, i: (bi, i, 0)),
                  pl.BlockSpec((1, n), lambda bi, i: (0, 0))],
        out_specs=pl.BlockSpec((None, tm, n), lambda bi, i: (bi, i, 0)),
        out_shape=jax.ShapeDtypeStruct((b, s, n), F32),
        compiler_params=_params(2, 8 * tm * n * 4),
        name="ssd_dt",
    )(dt_raw, dt_bias_pad.reshape(1, n))


def _split3(a):
    hi = a.astype(BF16)
    r1 = a - hi.astype(F32)
    mid = r1.astype(BF16)
    lo = (r1 - mid.astype(F32)).astype(BF16)
    return hi, mid, lo


def _ssd_scan_kernel(x_ref, b_ref, c_ref, dtc_ref, dtr_ref, ac_ref, ar_ref, dsk_ref, o_ref,
                     st_scr):
    ci = pl.program_id(2)
    L = x_ref.shape[0]

    @pl.when(ci == 0)
    def _():
        st_scr[...] = jnp.zeros_like(st_scr)

    li = lax.broadcasted_iota(jnp.int32, (L, L), 0)
    si = lax.broadcasted_iota(jnp.int32, (L, L), 1)
    causal = li >= si
    tril = jnp.where(causal, 1.0, 0.0).astype(BF16)
    triu = jnp.where(li <= si, 1.0, 0.0).astype(BF16)

    dtc = dtc_ref[...]
    adt_c = dtc * -jnp.exp(ac_ref[...])
    adt_cp = jnp.concatenate([adt_c, jnp.zeros((L, LANE - SSD_HPG), F32)], axis=1)
    acum_c = sum(_dot(tril, part) for part in _split3(adt_cp))
    adt_r = dtr_ref[...] * -jnp.exp(ar_ref[...])
    acum_r = sum(_dot(part, triu) for part in _split3(adt_r))

    xs = x_ref[...]
    bm = b_ref[...]
    cm = c_ref[...].astype(BF16)
    cb = _dot_nt(cm, bm.astype(BF16))
    state = st_scr[...]
    cs = _dot(cm, state.astype(BF16))
    bmt = jnp.transpose(bm).astype(BF16)

    ys = []
    xws = []
    decays = []
    for h in range(SSD_HPG):
        cols = slice(h * SSD_HEADDIM, (h + 1) * SSD_HEADDIM)
        ac_h = acum_c[:, h:h + 1]
        ar_h = acum_r[h:h + 1, :]
        tot_h = acum_c[L - 1:L, h:h + 1]
        lmat = jnp.exp(jnp.where(causal, ac_h - ar_h, NEG))
        xdt = xs[:, cols] * dtc[:, h:h + 1]
        y_diag = _dot((cb * lmat).astype(BF16), xdt.astype(BF16))
        y_off = cs[:, cols] * jnp.exp(ac_h)
        ys.append(y_diag + y_off)
        xws.append((xdt * jnp.exp(tot_h - ac_h)).astype(BF16))
        decays.append(jnp.broadcast_to(jnp.exp(tot_h), (1, SSD_HEADDIM)))
    y = jnp.concatenate(ys, axis=1)
    o_ref[...] = y + xs * dsk_ref[...]
    xw = jnp.concatenate(xws, axis=1)
    decay = jnp.concatenate(decays, axis=1)
    st_scr[...] = state * decay + _dot(bmt, xw)


def _ssd_scan(xbc, dtp, a_log, d_skip):
    b, s, _ = xbc.shape
    L = _pick(s, L_SSD)
    g = SSD_GROUPS
    dt_g = dtp.reshape(b, s, g, SSD_HPG)
    dtc = jnp.transpose(dt_g, (0, 2, 1, 3))
    dtr = jnp.transpose(dt_g, (0, 2, 3, 1))
    a_col = a_log.reshape(g, 1, SSD_HPG)
    a_row = a_log.reshape(g, SSD_HPG, 1)
    dsk = jnp.repeat(d_skip, SSD_HEADDIM).reshape(g, 1, SSD_GW)
    xoff = 0
    boff = SSD_INNER // SSD_STATE
    coff = boff + g
    vmem = 4 * L * SSD_GW * 4 + 24 * L * L * 4 + 16 * L * SSD_GW * 4
    return pl.pallas_call(
        _ssd_scan_kernel,
        grid=(b, g, s // L),
        in_specs=[pl.BlockSpec((None, L, SSD_GW), lambda bi, gi, c: (bi, c, xoff + gi)),
                  pl.BlockSpec((None, L, SSD_STATE), lambda bi, gi, c: (bi, c, boff + gi)),
                  pl.BlockSpec((None, L, SSD_STATE), lambda bi, gi, c: (bi, c, coff + gi)),
                  pl.BlockSpec((None, None, L, SSD_HPG), lambda bi, gi, c: (bi, gi, c, 0)),
                  pl.BlockSpec((None, None, SSD_HPG, L), lambda bi, gi, c: (bi, gi, 0, c)),
                  pl.BlockSpec((None, 1, SSD_HPG), lambda bi, gi, c: (gi, 0, 0)),
                  pl.BlockSpec((None, SSD_HPG, 1), lambda bi, gi, c: (gi, 0, 0)),
                  pl.BlockSpec((None, 1, SSD_GW), lambda bi, gi, c: (gi, 0, 0))],
        out_specs=pl.BlockSpec((None, L, SSD_GW), lambda bi, gi, c: (bi, c, gi)),
        out_shape=jax.ShapeDtypeStruct((b, s, SSD_INNER), F32),
        scratch_shapes=[pltpu.VMEM((SSD_STATE, SSD_GW), F32)],
        compiler_params=_params(3, vmem),
        name="ssd_scan",
    )(xbc, xbc, xbc, dtc, dtr, a_col, a_row, dsk)


def _gate_norm_kernel(y_ref, z_ref, g_ref, o_ref):
    v = y_ref[...] * _silu(z_ref[...].astype(F32))
    o_ref[...] = _rms_rows(v, g_ref[...]).astype(o_ref.dtype)


def _gate_norm(y, z, g):
    b, s, n = y.shape
    tm = _pick(s, 256)
    return pl.pallas_call(
        _gate_norm_kernel,
        grid=(b, s // tm),
        in_specs=[pl.BlockSpec((None, tm, n), lambda bi, i: (bi, i, 0)),
                  pl.BlockSpec((None, tm, n), lambda bi, i: (bi, i, 0)),
                  pl.BlockSpec((1, n), lambda bi, i: (0, 0))],
        out_specs=pl.BlockSpec((None, tm, n), lambda bi, i: (bi, i, 0)),
        out_shape=jax.ShapeDtypeStruct((b, s, n), BF16),
        compiler_params=_params(2, 8 * tm * n * 4),
        name="ssd_gate_norm",
    )(y, z, g.reshape(1, n))


def _final_norm_kernel(x_ref, g_ref, o_ref):
    o_ref[...] = _rms_rows(x_ref[...], g_ref[...])


def _final_norm(x, g):
    b, s, n = x.shape
    tm = _pick(s, 512)
    return pl.pallas_call(
        _final_norm_kernel,
        grid=(b, s // tm),
        in_specs=[pl.BlockSpec((None, tm, n), lambda bi, i: (bi, i, 0)),
                  pl.BlockSpec((1, n), lambda bi, i: (0, 0))],
        out_specs=pl.BlockSpec((None, tm, n), lambda bi, i: (bi, i, 0)),
        out_shape=jax.ShapeDtypeStruct((b, s, n), F32),
        compiler_params=_params(2, 8 * tm * n * 4),
        name="final_norm",
    )(x, g.reshape(1, n))


def _rope_tables(s):
    pos = jnp.arange(s, dtype=jnp.int32).astype(F32)[:, None]
    half = RET_DK // 2
    inv = RET_THETA ** (-jnp.arange(half, dtype=F32) / half)
    ang = pos * inv[None, :]
    ret_cos, ret_sin = jnp.cos(ang), jnp.sin(ang)
    hr = MLA_ROPE // 2
    inv_m = MLA_THETA ** (-jnp.arange(hr, dtype=F32) / hr)
    ang_m = pos * inv_m[None, :]
    cm, sm = jnp.cos(ang_m), jnp.sin(ang_m)
    pad = LANE - MLA_ROPE
    mla_cos = jnp.concatenate([cm, cm, jnp.ones((s, pad), F32)], axis=1)
    mla_sin = jnp.concatenate([-sm, sm, jnp.zeros((s, pad), F32)], axis=1)
    return ret_cos, ret_sin, mla_cos, mla_sin


def _hybrid_layer(x, mods, norm_g, w_in, q_g, w_uq, kv_g, w_ukv, gn_g, w_out, tables):
    sh, sc, gt = mods
    ret_cos, ret_sin, mla_cos, mla_sin = tables
    w_in = w_in.astype(BF16)
    o_v = 2 * RET_QK_W
    o_c = o_v + 2 * RET_V_W
    o_kr = o_c + 2 * MLA_RANK
    w_small = jnp.concatenate(
        [w_in[:, o_c:o_kr + MLA_ROPE], jnp.zeros((D_MODEL, LANE - MLA_ROPE), BF16)], axis=1)
    qk = _norm_mod_matmul(x, norm_g, sh, sc, w_in[:, :o_v], F32, "hyb_in_qk")
    vg = _norm_mod_matmul(x, norm_g, sh, sc, w_in[:, o_v:o_c], BF16, "hyb_in_vg")
    small = _norm_mod_matmul(x, norm_g, sh, sc, w_small, F32, "hyb_in_latent")
    y_ret = _retention(qk, vg, ret_cos, ret_sin, gn_g)

    wq = w_uq.astype(BF16).reshape(MLA_RANK, MLA_HEADS, MLA_NOPE + MLA_ROPE)
    wq = jnp.pad(wq, ((0, 0), (0, 0), (0, MLA_QK_PAD - MLA_NOPE - MLA_ROPE)))
    wq = wq.reshape(MLA_RANK, MLA_HEADS * MLA_QK_PAD)
    wkv = w_ukv.astype(BF16).reshape(MLA_RANK, MLA_HEADS, MLA_NOPE + MLA_V)
    wk = wkv[:, :, :MLA_NOPE].reshape(MLA_RANK, MLA_HEADS * MLA_NOPE)
    wv = wkv[:, :, MLA_NOPE:].reshape(MLA_RANK, MLA_HEADS * MLA_V)
    q, k, v = _mla_up(small, q_g, kv_g, wq, wk, wv, mla_cos, mla_sin)
    y_mla = _mla_attention(q, k, v)
    return _matmul_res([y_ret, y_mla], w_out.astype(BF16), x, gt, "hyb_out")


def _ssd_layer(x, mods, norm_g, w_in, conv_w, conv_b, dt_bias, a_log, d_skip, ssd_g, w_out):
    sh, sc, gt = mods
    w_in = w_in.astype(BF16)
    o_x = SSD_INNER
    o_dt = o_x + SSD_CONV_DIM
    w_dt = jnp.concatenate(
        [w_in[:, o_dt:], jnp.zeros((D_MODEL, LANE - SSD_HEADS), BF16)], axis=1)
    z = _norm_mod_matmul(x, norm_g, sh, sc, w_in[:, :o_x], BF16, "ssd_in_z")
    xbc = _norm_mod_matmul(x, norm_g, sh, sc, w_in[:, o_x:o_dt], F32, "ssd_in_xbc")
    dt_raw = _norm_mod_matmul(x, norm_g, sh, sc, w_dt, F32, "ssd_in_dt")
    xbc = _ssd_prep(xbc, conv_w, conv_b)
    dtp = _ssd_dt(dt_raw, jnp.pad(dt_bias, (0, LANE - SSD_HEADS)))[:, :, :SSD_HEADS]
    y = _ssd_scan(xbc, dtp, a_log, d_skip)
    hn = _gate_norm(y, z, ssd_g)
    return _matmul_res([hn], w_out.astype(BF16), x, gt, "ssd_out")


def _ffn_layer(x, mods, norm_g, w_up, conv_w, conv_b, w_down):
    sh, sc, gt = mods
    a = _ffn_up(x, norm_g, sh, sc, w_up.astype(BF16), conv_w, conv_b)
    return _matmul_res([a], w_down.astype(BF16), x, gt, "ffn_down")


def kernel(x, c, ada_w, ada_b, norm_mix_g, norm_ffn_g, hyb_w_in, hyb_q_norm_g, hyb_w_uq, hyb_kv_norm_g, hyb_w_ukv, hyb_ret_gn_g, hyb_w_out, ssd_w_in, ssd_conv_w, ssd_conv_b, ssd_dt_bias, ssd_a_log, ssd_d, ssd_norm_g, ssd_w_out, ffn_w_up, ffn_conv_w, ffn_conv_b, ffn_w_down, final_norm_g):
    b, s, d = x.shape
    depth = ada_w.shape[0]
    mods = _mods(c, ada_w, ada_b).reshape(depth, b, 6, 1, d)
    tables = _rope_tables(s)
    for l in range(depth):
        m = [mods[l, :, k] for k in range(6)]
        i = l // 2
        if l % 2 == 0:
            x = _hybrid_layer(x, m[0:3], norm_mix_g[l], hyb_w_in[i], hyb_q_norm_g[i], hyb_w_uq[i],
                              hyb_kv_norm_g[i], hyb_w_ukv[i], hyb_ret_gn_g[i], hyb_w_out[i], tables)
        else:
            x = _ssd_layer(x, m[0:3], norm_mix_g[l], ssd_w_in[i], ssd_conv_w[i], ssd_conv_b[i],
                           ssd_dt_bias[i], ssd_a_log[i], ssd_d[i], ssd_norm_g[i], ssd_w_out[i])
        x = _ffn_layer(x, m[3:6], norm_ffn_g[l], ffn_w_up[l], ffn_conv_w[l], ffn_conv_b[l],
                       ffn_w_down[l])
    return _final_norm(x, final_norm_g)
```

```python
import functools
import math

import numpy as np
import jax
import jax.numpy as jnp
from jax import lax
from jax.experimental import pallas as pl
from jax.experimental.pallas import tpu as pltpu

F32 = jnp.float32
BF16 = jnp.bfloat16

D_MODEL = 2048
DEPTH = 4
EPS = 1e-6

RET_HEADS = 4
RET_DK = 256
RET_DV = 512
RET_THETA = 10000.0
RET_QK_W = RET_HEADS * RET_DK
RET_V_W = RET_HEADS * RET_DV

MLA_HEADS = 16
MLA_NOPE = 128
MLA_ROPE = 64
MLA_V = 128
MLA_RANK = 512
MLA_THETA = 10000.0
MLA_QK_PAD = 256
MASK_CHUNK = 64

SSD_INNER = 4096
SSD_HEADDIM = 64
SSD_HEADS = 64
SSD_GROUPS = 8
SSD_HPG = SSD_HEADS // SSD_GROUPS
SSD_STATE = 128
SSD_CONV = 4
SSD_CONV_DIM = SSD_INNER + 2 * SSD_GROUPS * SSD_STATE
SSD_GW = SSD_HPG * SSD_HEADDIM

FFN_HIDDEN = 5632
FFN_CONV = 3

LANE = 128
SUBLANE = 8
VMEM_CAP = 56 * 1024 * 1024
NEG = -1e30

TM_PROJ = 1024
TN_PROJ = 1024
TH_FFN = 512
TN_RES = 512
L_RET = 256
L_SSD = 256
SSD_GROUPS_PER_STEP = 2
TQ_ATTN = 1024
TK_ATTN = 512
TM_UP = 512
TM_CONV = 512


def _pick(n, pref):
    t = pref
    while t > 1 and n % t:
        t //= 2
    return t if n % t == 0 else n


def _params(n_axes, vmem_bytes):
    limit = int(min(VMEM_CAP, max(16 * 1024 * 1024, vmem_bytes * 1.3 + (4 << 20))))
    return pltpu.CompilerParams(dimension_semantics=("arbitrary",) * n_axes,
                                vmem_limit_bytes=limit)


def _silu(v):
    return v * (1.0 / (1.0 + jnp.exp(-v)))


def _dot(a, b):
    return jnp.dot(a, b, preferred_element_type=F32)


def _dot_nt(a, b):
    return lax.dot_general(a, b, (((1,), (1,)), ((), ())), preferred_element_type=F32)


def _shift_rows(g, k, prev8):
    c = g.shape[1]
    r = pltpu.roll(g, k, 0)
    hr = pltpu.roll(prev8, k, 0)
    rows = lax.broadcasted_iota(jnp.int32, (SUBLANE, c), 0)
    top = jnp.where(rows < k, hr, r[0:SUBLANE])
    return jnp.concatenate([top, r[SUBLANE:]], axis=0)


def _causal_conv(g, w_ref, b_ref, prev8):
    width = w_ref.shape[0]
    y = g * w_ref[width - 1:width, :] + b_ref[...]
    for k in range(1, width):
        y = y + _shift_rows(g, k, prev8) * w_ref[width - 1 - k:width - k, :]
    return y


def _mods_kernel(c_ref, w_ref, b_ref, o_ref):
    h = _silu(c_ref[...]).astype(BF16)
    o_ref[...] = _dot(h, w_ref[...].astype(BF16)) + b_ref[...]


def _mods(c, ada_w, ada_b):
    depth, d, n = ada_w.shape
    b = c.shape[0]
    tn = _pick(n, 1024)
    return pl.pallas_call(
        _mods_kernel,
        grid=(depth, n // tn),
        in_specs=[pl.BlockSpec((b, d), lambda l, j: (0, 0)),
                  pl.BlockSpec((None, d, tn), lambda l, j: (l, 0, j)),
                  pl.BlockSpec((None, 1, tn), lambda l, j: (l, 0, j))],
        out_specs=pl.BlockSpec((None, b, tn), lambda l, j: (l, 0, j)),
        out_shape=jax.ShapeDtypeStruct((depth, b, n), F32),
        compiler_params=_params(2, 2 * d * tn * 4 + d * tn * 2),
        name="adaln_mods",
    )(c, ada_w, ada_b.reshape(depth, 1, n))


def _norm_mod_rows(x_ref, g_ref, sh_ref, sc_ref, h_scr):
    tm = x_ref.shape[0]
    rc = _pick(tm, 256)
    g = g_ref[...]
    sc1 = 1.0 + sc_ref[...]
    sh = sh_ref[...]

    def body(r, carry):
        rows = pl.ds(pl.multiple_of(r * rc, rc), rc)
        xf = x_ref[rows, :]
        inv = lax.rsqrt(jnp.mean(xf * xf, axis=-1, keepdims=True) + EPS)
        h_scr[rows, :] = ((xf * inv * g) * sc1 + sh).astype(BF16)
        return carry

    lax.fori_loop(0, tm // rc, body, 0)


def _proj_kernel(x_ref, g_ref, sh_ref, sc_ref, w_ref, o_ref, h_scr):
    @pl.when(pl.program_id(2) == 0)
    def _():
        _norm_mod_rows(x_ref, g_ref, sh_ref, sc_ref, h_scr)

    o_ref[...] = _dot(h_scr[...], w_ref[...]).astype(o_ref.dtype)


def _norm_mod_matmul(x, g, shift, scale, w, out_dtype, name):
    b, s, d = x.shape
    n = w.shape[1]
    tm = _pick(s, TM_PROJ)
    tn = _pick(n, TN_PROJ) if n % TN_PROJ == 0 else n
    osz = jnp.dtype(out_dtype).itemsize
    vmem = 2 * tm * d * 4 + tm * d * 2 + 2 * d * tn * 2 + 2 * tm * tn * osz + tm * tn * 4
    return pl.pallas_call(
        _proj_kernel,
        grid=(b, s // tm, n // tn),
        in_specs=[pl.BlockSpec((None, tm, d), lambda bi, i, j: (bi, i, 0)),
                  pl.BlockSpec((1, d), lambda bi, i, j: (0, 0)),
                  pl.BlockSpec((None, 1, d), lambda bi, i, j: (bi, 0, 0)),
                  pl.BlockSpec((None, 1, d), lambda bi, i, j: (bi, 0, 0)),
                  pl.BlockSpec((d, tn), lambda bi, i, j: (0, j))],
        out_specs=pl.BlockSpec((None, tm, tn), lambda bi, i, j: (bi, i, j)),
        out_shape=jax.ShapeDtypeStruct((b, s, n), out_dtype),
        scratch_shapes=[pltpu.VMEM((tm, d), BF16)],
        compiler_params=_params(3, vmem),
        name=name,
    )(x, g.reshape(1, d), shift, scale, w)


def _ffn_up_kernel(x_ref, g_ref, sh_ref, sc_ref, wu_ref, wg_ref, cw_ref, cb_ref, o_ref,
                   h_scr, halo_scr):
    si = pl.program_id(1)
    j = pl.program_id(2)

    @pl.when(j == 0)
    def _():
        _norm_mod_rows(x_ref, g_ref, sh_ref, sc_ref, h_scr)

    h = h_scr[...]
    u = _dot(h, wu_ref[...])
    gate = _dot(h, wg_ref[...])
    tm = gate.shape[0]
    prev8 = jnp.where(si == 0, 0.0, halo_scr[j])
    halo_scr[j] = gate[tm - SUBLANE:tm, :]
    gc = _causal_conv(gate, cw_ref, cb_ref, prev8)
    o_ref[...] = (_silu(gc) * u).astype(o_ref.dtype)


def _ffn_up(x, g, shift, scale, w_up, conv_w, conv_b):
    b, s, d = x.shape
    hid = conv_w.shape[1]
    tm = _pick(s, TM_PROJ)
    th = _pick(hid, TH_FFN)
    nj = hid // th
    vmem = (2 * tm * d * 4 + tm * d * 2 + 4 * d * th * 2 + 2 * tm * th * 2
            + 6 * tm * th * 4)
    return pl.pallas_call(
        _ffn_up_kernel,
        grid=(b, s // tm, nj),
        in_specs=[pl.BlockSpec((None, tm, d), lambda bi, i, j: (bi, i, 0)),
                  pl.BlockSpec((1, d), lambda bi, i, j: (0, 0)),
                  pl.BlockSpec((None, 1, d), lambda bi, i, j: (bi, 0, 0)),
                  pl.BlockSpec((None, 1, d), lambda bi, i, j: (bi, 0, 0)),
                  pl.BlockSpec((d, th), lambda bi, i, j: (0, j)),
                  pl.BlockSpec((d, th), lambda bi, i, j: (0, j + nj)),
                  pl.BlockSpec((FFN_CONV, th), lambda bi, i, j: (0, j)),
                  pl.BlockSpec((1, th), lambda bi, i, j: (0, j))],
        out_specs=pl.BlockSpec((None, tm, th), lambda bi, i, j: (bi, i, j)),
        out_shape=jax.ShapeDtypeStruct((b, s, hid), BF16),
        scratch_shapes=[pltpu.VMEM((tm, d), BF16),
                        pltpu.VMEM((nj, SUBLANE, th), F32)],
        compiler_params=_params(3, vmem),
        name="ffn_up",
    )(x, g.reshape(1, d), shift, scale, w_up, w_up, conv_w, conv_b.reshape(1, hid))


def _matmul_res_kernel(*refs, n_a):
    a_refs = refs[:n_a]
    w_refs = refs[n_a:2 * n_a]
    x_ref, gt_ref, o_ref = refs[2 * n_a:]
    acc = _dot(a_refs[0][...], w_refs[0][...])
    for a_ref, w_ref in zip(a_refs[1:], w_refs[1:]):
        acc = acc + _dot(a_ref[...], w_ref[...])
    o_ref[...] = x_ref[...] + gt_ref[...] * acc


def _matmul_res(a_list, w, x, gate, name):
    b, s, n = x.shape
    ks = [a.shape[2] for a in a_list]
    tm = _pick(s, TM_PROJ)
    tn = _pick(n, TN_RES)
    in_specs = [pl.BlockSpec((None, tm, k), lambda bi, i, j: (bi, i, 0)) for k in ks]
    assert len(set(ks)) == 1
    in_specs += [pl.BlockSpec((ks[0], tn), functools.partial(lambda bi, i, j, p: (p, j), p=p))
                 for p in range(len(ks))]
    in_specs += [pl.BlockSpec((None, tm, tn), lambda bi, i, j: (bi, i, j)),
                 pl.BlockSpec((None, 1, tn), lambda bi, i, j: (bi, 0, j))]
    ktot = sum(ks)
    vmem = 2 * tm * ktot * 2 + 2 * ktot * tn * 2 + 5 * tm * tn * 4
    return pl.pallas_call(
        functools.partial(_matmul_res_kernel, n_a=len(a_list)),
        grid=(b, s // tm, n // tn),
        in_specs=in_specs,
        out_specs=pl.BlockSpec((None, tm, tn), lambda bi, i, j: (bi, i, j)),
        out_shape=jax.ShapeDtypeStruct((b, s, n), F32),
        compiler_params=_params(3, vmem),
        name=name,
    )(*a_list, *([w] * len(a_list)), x, gate)


def _ret_log_gamma(h):
    return math.log1p(-(2.0 ** (-5.0 - h)))


def _retention_kernel(q_ref, k_ref, v_ref, g_ref, cos_ref, sin_ref, gn_ref, o_ref, r_scr):
    ci = pl.program_id(1)
    L = q_ref.shape[0]

    @pl.when(ci == 0)
    def _():
        r_scr[...] = jnp.zeros_like(r_scr)

    cos = cos_ref[...]
    sin = sin_ref[...]
    li = lax.broadcasted_iota(jnp.int32, (L, L), 0)
    si = lax.broadcasted_iota(jnp.int32, (L, L), 1)
    rel = (li - si).astype(F32)
    causal = li >= si
    pos = lax.broadcasted_iota(jnp.int32, (L, 1), 0).astype(F32)
    half = RET_DK // 2

    def rope(t):
        t1 = t[:, :half]
        t2 = t[:, half:]
        return jnp.concatenate([t1 * cos - t2 * sin, t1 * sin + t2 * cos], axis=1)

    for h in range(RET_HEADS):
        lg = _ret_log_gamma(h)
        q = rope(q_ref[:, h * RET_DK:(h + 1) * RET_DK])
        k = rope(k_ref[:, h * RET_DK:(h + 1) * RET_DK]) * (RET_DK ** -0.5)
        v = v_ref[:, h * RET_DV:(h + 1) * RET_DV].astype(F32)
        qb = q.astype(BF16)
        decay = jnp.where(causal, jnp.exp(jnp.where(causal, rel, 0.0) * lg), 0.0)
        inner = _dot_nt(qb, k.astype(BF16)) * decay
        xi = jnp.exp(lg * (pos + 1.0))
        zeta = jnp.exp(lg * (L - 1.0 - pos))
        r_old = r_scr[h]
        y = _dot(inner.astype(BF16), v.astype(BF16)) + _dot(qb, r_old.astype(BF16)) * xi
        kt = jnp.transpose(k).astype(BF16)
        r_scr[h] = math.exp(lg * L) * r_old + _dot(kt, (v * zeta).astype(BF16))
        yc = y - jnp.mean(y, axis=-1, keepdims=True)
        yn = yc * lax.rsqrt(jnp.mean(yc * yc, axis=-1, keepdims=True) + EPS)
        cols = slice(h * RET_DV, (h + 1) * RET_DV)
        gate = _silu(g_ref[:, cols].astype(F32))
        o_ref[:, cols] = (gate * (yn * gn_ref[:, cols])).astype(o_ref.dtype)


def _retention(qk, vg, cos, sin, gn_g):
    b, s, _ = qk.shape
    L = _pick(s, L_RET)
    vmem = 2 * (2 * L * RET_QK_W * 4 + 2 * L * RET_V_W * 2 + L * RET_V_W * 2) \
        + RET_HEADS * RET_DK * RET_DV * 4 + 24 * L * RET_DV * 4
    return pl.pallas_call(
        _retention_kernel,
        grid=(b, s // L),
        in_specs=[pl.BlockSpec((None, L, RET_QK_W), lambda bi, c: (bi, c, 0)),
                  pl.BlockSpec((None, L, RET_QK_W), lambda bi, c: (bi, c, 1)),
                  pl.BlockSpec((None, L, RET_V_W), lambda bi, c: (bi, c, 0)),
                  pl.BlockSpec((None, L, RET_V_W), lambda bi, c: (bi, c, 1)),
                  pl.BlockSpec((L, RET_DK // 2), lambda bi, c: (c, 0)),
                  pl.BlockSpec((L, RET_DK // 2), lambda bi, c: (c, 0)),
                  pl.BlockSpec((1, RET_V_W), lambda bi, c: (0, 0))],
        out_specs=pl.BlockSpec((None, L, RET_V_W), lambda bi, c: (bi, c, 0)),
        out_shape=jax.ShapeDtypeStruct((b, s, RET_V_W), BF16),
        scratch_shapes=[pltpu.VMEM((RET_HEADS, RET_DK, RET_DV), F32)],
        compiler_params=_params(2, vmem),
        name="retention",
    )(qk, qk, vg, vg, cos, sin, gn_g.reshape(1, RET_V_W))


def _rms_rows(x, g):
    return x * lax.rsqrt(jnp.mean(x * x, axis=-1, keepdims=True) + EPS) * g


def _rope_slot(t, cos_t, sin_t):
    lane = lax.broadcasted_iota(jnp.int32, t.shape, 1)
    hr = MLA_ROPE // 2
    swapped = jnp.where(lane < hr, pltpu.roll(t, LANE - hr, 1), pltpu.roll(t, hr, 1))
    return t * cos_t + swapped * sin_t


def _q_up_kernel(c_ref, g_ref, w_ref, cos_ref, sin_ref, o_ref):
    h = _rms_rows(c_ref[...], g_ref[...]).astype(BF16)
    q = _dot(h, w_ref[...]) * MLA_Q_SCALE
    cos_t = cos_ref[...]
    sin_t = sin_ref[...]
    for hd in range(MLA_HEADS):
        base = hd * MLA_QK_PAD
        o_ref[:, base:base + MLA_NOPE] = q[:, base:base + MLA_NOPE].astype(o_ref.dtype)
        rs = q[:, base + MLA_NOPE:base + MLA_QK_PAD]
        o_ref[:, base + MLA_NOPE:base + MLA_QK_PAD] = _rope_slot(rs, cos_t, sin_t).astype(o_ref.dtype)


def _kv_up_kernel(c_ref, kr_ref, g_ref, wk_ref, wv_ref, cos_ref, sin_ref, k_ref, v_ref):
    h = _rms_rows(c_ref[...], g_ref[...]).astype(BF16)
    kn = _dot(h, wk_ref[...])
    v_ref[...] = _dot(h, wv_ref[...]).astype(v_ref.dtype)
    kr = _rope_slot(kr_ref[...], cos_ref[...], sin_ref[...]).astype(k_ref.dtype)
    for hd in range(MLA_HEADS):
        base = hd * MLA_QK_PAD
        k_ref[:, base:base + MLA_NOPE] = kn[:, hd * MLA_NOPE:(hd + 1) * MLA_NOPE].astype(k_ref.dtype)
        k_ref[:, base + MLA_NOPE:base + MLA_QK_PAD] = kr


def _mla_up(small, q_g, kv_g, wq, wk, wv, cos_t, sin_t):
    b, s, _ = small.shape
    tm = _pick(s, TM_UP)
    nq = MLA_HEADS * MLA_QK_PAD
    nv = MLA_HEADS * MLA_V
    tab = pl.BlockSpec((tm, LANE), lambda bi, i: (i, 0))
    q = pl.pallas_call(
        _q_up_kernel,
        grid=(b, s // tm),
        in_specs=[pl.BlockSpec((None, tm, MLA_RANK), lambda bi, i: (bi, i, 0)),
                  pl.BlockSpec((1, MLA_RANK), lambda bi, i: (0, 0)),
                  pl.BlockSpec((MLA_RANK, nq), lambda bi, i: (0, 0)),
                  tab, tab],
        out_specs=pl.BlockSpec((None, tm, nq), lambda bi, i: (bi, i, 0)),
        out_shape=jax.ShapeDtypeStruct((b, s, nq), BF16),
        compiler_params=_params(2, 2 * MLA_RANK * nq * 2 + 2 * tm * nq * 2 + 3 * tm * nq * 4),
        name="mla_q_up",
    )(small, q_g.reshape(1, MLA_RANK), wq, cos_t, sin_t)
    k, v = pl.pallas_call(
        _kv_up_kernel,
        grid=(b, s // tm),
        in_specs=[pl.BlockSpec((None, tm, MLA_RANK), lambda bi, i: (bi, i, 1)),
                  pl.BlockSpec((None, tm, LANE), lambda bi, i: (bi, i, 2 * MLA_RANK // LANE)),
                  pl.BlockSpec((1, MLA_RANK), lambda bi, i: (0, 0)),
                  pl.BlockSpec((MLA_RANK, nv), lambda bi, i: (0, 0)),
                  pl.BlockSpec((MLA_RANK, nv), lambda bi, i: (0, 0)),
                  tab, tab],
        out_specs=[pl.BlockSpec((None, tm, nq), lambda bi, i: (bi, i, 0)),
                   pl.BlockSpec((None, tm, nv), lambda bi, i: (bi, i, 0))],
        out_shape=[jax.ShapeDtypeStruct((b, s, nq), BF16),
                   jax.ShapeDtypeStruct((b, s, nv), BF16)],
        compiler_params=_params(2, 4 * MLA_RANK * nv * 2 + 2 * tm * (nq + nv) * 2 + 3 * tm * nq * 4),
        name="mla_kv_up",
    )(small, small, kv_g.reshape(1, MLA_RANK), wk, wv, cos_t, sin_t)
    return q, k, v


MLA_Q_SCALE = ((MLA_NOPE + MLA_ROPE) ** -0.5) * math.log2(math.e)


def _mla_attn_kernel(q_ref, k_ref, v_ref, o_ref, *, tk):
    qi = pl.program_id(2)
    tq = q_ref.shape[0]
    q = q_ref[...]

    def update(carry, s2, v):
        m, l, acc = carry
        m_new = jnp.maximum(m, jnp.max(s2, axis=-1, keepdims=True))
        p = jnp.exp2(s2 - m_new)
        alpha = jnp.exp2(m - m_new)
        l = alpha * l + jnp.sum(p, axis=-1, keepdims=True)
        acc = alpha * acc + _dot(p.astype(BF16), v)
        return m_new, l, acc

    def body(ki, carry):
        rows = pl.ds(pl.multiple_of(ki * tk, tk), tk)
        return update(carry, _dot_nt(q, k_ref[rows, :]), v_ref[rows, :])

    init = (jnp.full((tq, 1), NEG, F32), jnp.zeros((tq, 1), F32), jnp.zeros((tq, MLA_V), F32))
    carry = lax.fori_loop(0, qi * (tq // tk), body, init)

    qc = lax.broadcasted_iota(jnp.int32, (tq, tk), 0) // MASK_CHUNK
    kc = lax.broadcasted_iota(jnp.int32, (tq, tk), 1) // MASK_CHUNK
    for j in range(tq // tk):
        rows = pl.ds(pl.multiple_of(qi * tq + j * tk, tk), tk)
        s2 = jnp.where(kc + (j * tk) // MASK_CHUNK <= qc, _dot_nt(q, k_ref[rows, :]), NEG)
        carry = update(carry, s2, v_ref[rows, :])
    m, l, acc = carry
    o_ref[...] = (acc / l).astype(o_ref.dtype)


def _mla_attention(q, k, v):
    b, s, _ = q.shape
    tq = _pick(s, TQ_ATTN)
    tk = _pick(tq, TK_ATTN)
    vmem = 2 * (tq * MLA_QK_PAD * 2 + s * MLA_QK_PAD * 2 + s * MLA_V * 2 + tq * MLA_V * 2) \
        + 6 * tq * tk * 4
    return pl.pallas_call(
        functools.partial(_mla_attn_kernel, tk=tk),
        grid=(b, MLA_HEADS, s // tq),
        in_specs=[pl.BlockSpec((None, tq, MLA_QK_PAD), lambda bi, h, i: (bi, i, h)),
                  pl.BlockSpec((None, s, MLA_QK_PAD), lambda bi, h, i: (bi, 0, h)),
                  pl.BlockSpec((None, s, MLA_V), lambda bi, h, i: (bi, 0, h))],
        out_specs=pl.BlockSpec((None, tq, MLA_V), lambda bi, h, i: (bi, i, h)),
        out_shape=jax.ShapeDtypeStruct((b, s, MLA_HEADS * MLA_V), BF16),
        compiler_params=_params(3, vmem),
        name="mla_attention",
    )(q, k, v)


def _ssd_prep_kernel(x_ref, cw_ref, cb_ref, o_ref, halo_scr):
    si = pl.program_id(1)
    j = pl.program_id(2)
    x = x_ref[...]
    tm = x.shape[0]
    prev8 = jnp.where(si == 0, 0.0, halo_scr[j])
    halo_scr[j] = x[tm - SUBLANE:tm, :]
    o_ref[...] = _silu(_causal_conv(x, cw_ref, cb_ref, prev8))


def _ssd_prep(xbc, conv_w, conv_b):
    b, s, n = xbc.shape
    tm = _pick(s, TM_CONV)
    tc = _pick(n, 2048)
    nj = n // tc
    return pl.pallas_call(
        _ssd_prep_kernel,
        grid=(b, s // tm, nj),
        in_specs=[pl.BlockSpec((None, tm, tc), lambda bi, i, j: (bi, i, j)),
                  pl.BlockSpec((SSD_CONV, tc), lambda bi, i, j: (0, j)),
                  pl.BlockSpec((1, tc), lambda bi, i, j: (0, j))],
        out_specs=pl.BlockSpec((None, tm, tc), lambda bi, i, j: (bi, i, j)),
        out_shape=jax.ShapeDtypeStruct((b, s, n), F32),
        scratch_shapes=[pltpu.VMEM((nj, SUBLANE, tc), F32)],
        compiler_params=_params(3, 8 * tm * tc * 4),
        name="ssd_conv",
    )(xbc, conv_w, conv_b.reshape(1, n))


def _softplus_kernel(x_ref, b_ref, o_ref):
    v = x_ref[...] + b_ref[...]
    o_ref[...] = jnp.maximum(v, 0.0) + jnp.log1p(jnp.exp(-jnp.abs(v)))


def _ssd_dt(dt_raw, dt_bias_pad):
    b, s, n = dt_raw.shape
    tm = _pick(s, 1024)
    return pl.pallas_call(
        _softplus_kernel,
        grid=(b, s // tm),
        in_specs=[pl.BlockSpec((None, tm, n), lambda bi, i: (bi, i, 0)),
                  pl.BlockSpec((1, n), lambda bi, i: (0, 0))],
        out_specs=pl.BlockSpec((None, tm, n), lambda bi, i: (bi, i, 0)),
        out_shape=jax.ShapeDtypeStruct((b, s, n), F32),
        compiler_params=_params(2, 8 * tm * n * 4),
        name="ssd_dt",
    )(dt_raw, dt_bias_pad.reshape(1, n))


def _split3(a):
    hi = a.astype(BF16)
    r1 = a - hi.astype(F32)
    mid = r1.astype(BF16)
    lo = (r1 - mid.astype(F32)).astype(BF16)
    return hi, mid, lo


SSD_PIECES = 3
SSD_QW = SSD_PIECES * SSD_HPG
SSD_NCOPY = 3 * SSD_PIECES


def _ssd_expand_matrix():
    r = np.arange(LANE)[:, None]
    c = np.arange(3 * SSD_GW)[None, :]
    hit = (r < 3 * SSD_QW) & (c // SSD_GW == r // SSD_QW) & ((c % SSD_GW) // SSD_HEADDIM == r % SSD_HPG)
    return jnp.asarray(hit, dtype=BF16)


def _ssd_scan_kernel(x_ref, b_ref, c_ref, dtc_ref, dtr_ref, ac_ref, ar_ref, dsk_ref, ex_ref,
                     o_ref, st_scr):
    ci = pl.program_id(2)
    L = x_ref.shape[0]

    @pl.when(ci == 0)
    def _():
        st_scr[...] = jnp.zeros_like(st_scr)

    li = lax.broadcasted_iota(jnp.int32, (L, L), 0)
    si = lax.broadcasted_iota(jnp.int32, (L, L), 1)
    causal = li >= si
    tril = jnp.where(causal, 1.0, 0.0).astype(BF16)
    triu = jnp.where(li <= si, 1.0, 0.0).astype(BF16)
    pad = jnp.zeros((L, LANE - SSD_NCOPY * SSD_HPG), F32)
    lane = lax.broadcasted_iota(jnp.int32, (L, LANE), 1)
    piece = (lane % SSD_QW) // SSD_HPG
    left = lane < SSD_HEADDIM

    for gi in range(dtc_ref.shape[0]):
        gcols = slice(gi * SSD_GW, (gi + 1) * SSD_GW)
        ncols = slice(gi * SSD_STATE, (gi + 1) * SSD_STATE)

        dt_c = jnp.concatenate([dtc_ref[gi], pad], axis=1)
        a_c = jnp.concatenate([-jnp.exp(ac_ref[gi]), pad[0:1]], axis=1)
        acum = sum(_dot(tril, part) for part in _split3(dt_c * a_c))
        tot = acum[L - 1:L, :]
        quantity = jnp.where(lane < SSD_QW, dt_c,
                             jnp.where(lane < 2 * SSD_QW, jnp.exp(acum), jnp.exp(tot - acum)))
        hi, mid, lo = (p.astype(F32) for p in _split3(quantity))
        cols = jnp.where(piece == 0, hi, jnp.where(piece == 1, mid, lo)).astype(BF16)
        expanded = _dot(cols, ex_ref[...])
        dt_x = expanded[:, 0:SSD_GW]
        ea_x = expanded[:, SSD_GW:2 * SSD_GW]
        te_x = expanded[:, 2 * SSD_GW:3 * SSD_GW]

        acum_r = sum(_dot(part, triu) for part in _split3(dtr_ref[gi] * -jnp.exp(ar_ref[gi])))

        xs = x_ref[:, gcols]
        bm = b_ref[:, ncols]
        cm = c_ref[:, ncols].astype(BF16)
        cb = _dot_nt(cm, bm.astype(BF16))
        state = st_scr[gi]
        cs = _dot(cm, state.astype(BF16))
        bmt = jnp.transpose(bm).astype(BF16)
        xdt = xs * dt_x

        ys = []
        for t in range(SSD_GW // LANE):
            x2 = xdt[:, t * LANE:(t + 1) * LANE]
            y_t = None
            for h, keep in ((2 * t, left), (2 * t + 1, jnp.logical_not(left))):
                seg = acum[:, h:h + 1] - acum_r[h:h + 1, :]
                lmat = jnp.exp(jnp.where(causal, seg, NEG))
                part = _dot((cb * lmat).astype(BF16), jnp.where(keep, x2, 0.0).astype(BF16))
                y_t = part if y_t is None else y_t + part
            ys.append(y_t)
        y_diag = jnp.concatenate(ys, axis=1)
        o_ref[:, gcols] = y_diag + cs * ea_x + xs * dsk_ref[:, gcols]
        st_scr[gi] = state * ea_x[L - 1:L, :] + _dot(bmt, (xdt * te_x).astype(BF16))


def _ssd_scan(xbc, dtp, a_log, d_skip):
    b, s, _ = xbc.shape
    L = _pick(s, L_SSD)
    g = SSD_GROUPS
    gps = SSD_GROUPS_PER_STEP
    dt_g = dtp.reshape(b, s, g, SSD_HPG)
    dtc = jnp.tile(jnp.transpose(dt_g, (0, 2, 1, 3)), (1, 1, 1, SSD_NCOPY))
    dtr = jnp.transpose(dt_g, (0, 2, 3, 1))
    a_col = jnp.tile(a_log.reshape(g, 1, SSD_HPG), (1, 1, SSD_NCOPY))
    a_row = a_log.reshape(g, SSD_HPG, 1)
    ncol = SSD_NCOPY * SSD_HPG
    dsk = jnp.repeat(d_skip, SSD_HEADDIM).reshape(1, SSD_INNER)
    xw = gps * SSD_GW
    nw = gps * SSD_STATE
    boff = SSD_INNER // nw
    coff = boff + g // gps
    vmem = gps * (4 * L * SSD_GW * 4 + 24 * L * L * 4 + 16 * L * SSD_GW * 4)
    return pl.pallas_call(
        _ssd_scan_kernel,
        grid=(b, g // gps, s // L),
        in_specs=[pl.BlockSpec((None, L, xw), lambda bi, gi, c: (bi, c, gi)),
                  pl.BlockSpec((None, L, nw), lambda bi, gi, c: (bi, c, boff + gi)),
                  pl.BlockSpec((None, L, nw), lambda bi, gi, c: (bi, c, coff + gi)),
                  pl.BlockSpec((None, gps, L, ncol), lambda bi, gi, c: (bi, gi, c, 0)),
                  pl.BlockSpec((None, gps, SSD_HPG, L), lambda bi, gi, c: (bi, gi, 0, c)),
                  pl.BlockSpec((gps, 1, ncol), lambda bi, gi, c: (gi, 0, 0)),
                  pl.BlockSpec((gps, SSD_HPG, 1), lambda bi, gi, c: (gi, 0, 0)),
                  pl.BlockSpec((1, xw), lambda bi, gi, c: (0, gi)),
                  pl.BlockSpec((LANE, 3 * SSD_GW), lambda bi, gi, c: (0, 0))],
        out_specs=pl.BlockSpec((None, L, xw), lambda bi, gi, c: (bi, c, gi)),
        out_shape=jax.ShapeDtypeStruct((b, s, SSD_INNER), F32),
        scratch_shapes=[pltpu.VMEM((gps, SSD_STATE, SSD_GW), F32)],
        compiler_params=_params(3, vmem),
        name="ssd_scan",
    )(xbc, xbc, xbc, dtc, dtr, a_col, a_row, dsk, _ssd_expand_matrix())


def _gate_norm_kernel(y_ref, z_ref, g_ref, o_ref):
    v = y_ref[...] * _silu(z_ref[...].astype(F32))
    o_ref[...] = _rms_rows(v, g_ref[...]).astype(o_ref.dtype)


def _gate_norm(y, z, g):
    b, s, n = y.shape
    tm = _pick(s, 256)
    return pl.pallas_call(
        _gate_norm_kernel,
        grid=(b, s // tm),
        in_specs=[pl.BlockSpec((None, tm, n), lambda bi, i: (bi, i, 0)),
                  pl.BlockSpec((None, tm, n), lambda bi, i: (bi, i, 0)),
                  pl.BlockSpec((1, n), lambda bi, i: (0, 0))],
        out_specs=pl.BlockSpec((None, tm, n), lambda bi, i: (bi, i, 0)),
        out_shape=jax.ShapeDtypeStruct((b, s, n), BF16),
        compiler_params=_params(2, 8 * tm * n * 4),
        name="ssd_gate_norm",
    )(y, z, g.reshape(1, n))


def _final_norm_kernel(x_ref, g_ref, o_ref):
    o_ref[...] = _rms_rows(x_ref[...], g_ref[...])


def _final_norm(x, g):
    b, s, n = x.shape
    tm = _pick(s, 512)
    return pl.pallas_call(
        _final_norm_kernel,
        grid=(b, s // tm),
        in_specs=[pl.BlockSpec((None, tm, n), lambda bi, i: (bi, i, 0)),
                  pl.BlockSpec((1, n), lambda bi, i: (0, 0))],
        out_specs=pl.BlockSpec((None, tm, n), lambda bi, i: (bi, i, 0)),
        out_shape=jax.ShapeDtypeStruct((b, s, n), F32),
        compiler_params=_params(2, 8 * tm * n * 4),
        name="final_norm",
    )(x, g.reshape(1, n))


def _rope_tables(s):
    pos = jnp.arange(s, dtype=jnp.int32).astype(F32)[:, None]
    half = RET_DK // 2
    inv = RET_THETA ** (-jnp.arange(half, dtype=F32) / half)
    ang = pos * inv[None, :]
    ret_cos, ret_sin = jnp.cos(ang), jnp.sin(ang)
    hr = MLA_ROPE // 2
    inv_m = MLA_THETA ** (-jnp.arange(hr, dtype=F32) / hr)
    ang_m = pos * inv_m[None, :]
    cm, sm = jnp.cos(ang_m), jnp.sin(ang_m)
    pad = LANE - MLA_ROPE
    mla_cos = jnp.concatenate([cm, cm, jnp.ones((s, pad), F32)], axis=1)
    mla_sin = jnp.concatenate([-sm, sm, jnp.zeros((s, pad), F32)], axis=1)
    return ret_cos, ret_sin, mla_cos, mla_sin


def _hybrid_layer(x, mods, norm_g, w_in, q_g, w_uq, kv_g, w_ukv, gn_g, w_out, tables):
    sh, sc, gt = mods
    ret_cos, ret_sin, mla_cos, mla_sin = tables
    w_in = w_in.astype(BF16)
    o_v = 2 * RET_QK_W
    o_c = o_v + 2 * RET_V_W
    o_kr = o_c + 2 * MLA_RANK
    w_small = jnp.concatenate(
        [w_in[:, o_c:o_kr + MLA_ROPE], jnp.zeros((D_MODEL, LANE - MLA_ROPE), BF16)], axis=1)
    qk = _norm_mod_matmul(x, norm_g, sh, sc, w_in[:, :o_v], F32, "hyb_in_qk")
    vg = _norm_mod_matmul(x, norm_g, sh, sc, w_in[:, o_v:o_c], BF16, "hyb_in_vg")
    small = _norm_mod_matmul(x, norm_g, sh, sc, w_small, F32, "hyb_in_latent")
    y_ret = _retention(qk, vg, ret_cos, ret_sin, gn_g)

    wq = w_uq.astype(BF16).reshape(MLA_RANK, MLA_HEADS, MLA_NOPE + MLA_ROPE)
    wq = jnp.pad(wq, ((0, 0), (0, 0), (0, MLA_QK_PAD - MLA_NOPE - MLA_ROPE)))
    wq = wq.reshape(MLA_RANK, MLA_HEADS * MLA_QK_PAD)
    wkv = w_ukv.astype(BF16).reshape(MLA_RANK, MLA_HEADS, MLA_NOPE + MLA_V)
    wk = wkv[:, :, :MLA_NOPE].reshape(MLA_RANK, MLA_HEADS * MLA_NOPE)
    wv = wkv[:, :, MLA_NOPE:].reshape(MLA_RANK, MLA_HEADS * MLA_V)
    q, k, v = _mla_up(small, q_g, kv_g, wq, wk, wv, mla_cos, mla_sin)
    y_mla = _mla_attention(q, k, v)
    return _matmul_res([y_ret, y_mla], w_out.astype(BF16), x, gt, "hyb_out")


def _ssd_layer(x, mods, norm_g, w_in, conv_w, conv_b, dt_bias, a_log, d_skip, ssd_g, w_out):
    sh, sc, gt = mods
    w_in = w_in.astype(BF16)
    o_x = SSD_INNER
    o_dt = o_x + SSD_CONV_DIM
    w_dt = jnp.concatenate(
        [w_in[:, o_dt:], jnp.zeros((D_MODEL, LANE - SSD_HEADS), BF16)], axis=1)
    z = _norm_mod_matmul(x, norm_g, sh, sc, w_in[:, :o_x], BF16, "ssd_in_z")
    xbc = _norm_mod_matmul(x, norm_g, sh, sc, w_in[:, o_x:o_dt], F32, "ssd_in_xbc")
    dt_raw = _norm_mod_matmul(x, norm_g, sh, sc, w_dt, F32, "ssd_in_dt")
    xbc = _ssd_prep(xbc, conv_w, conv_b)
    dtp = _ssd_dt(dt_raw, jnp.pad(dt_bias, (0, LANE - SSD_HEADS)))[:, :, :SSD_HEADS]
    y = _ssd_scan(xbc, dtp, a_log, d_skip)
    hn = _gate_norm(y, z, ssd_g)
    return _matmul_res([hn], w_out.astype(BF16), x, gt, "ssd_out")


def _ffn_layer(x, mods, norm_g, w_up, conv_w, conv_b, w_down):
    sh, sc, gt = mods
    a = _ffn_up(x, norm_g, sh, sc, w_up.astype(BF16), conv_w, conv_b)
    return _matmul_res([a], w_down.astype(BF16), x, gt, "ffn_down")


def kernel(x, c, ada_w, ada_b, norm_mix_g, norm_ffn_g, hyb_w_in, hyb_q_norm_g, hyb_w_uq, hyb_kv_norm_g, hyb_w_ukv, hyb_ret_gn_g, hyb_w_out, ssd_w_in, ssd_conv_w, ssd_conv_b, ssd_dt_bias, ssd_a_log, ssd_d, ssd_norm_g, ssd_w_out, ffn_w_up, ffn_conv_w, ffn_conv_b, ffn_w_down, final_norm_g):
    b, s, d = x.shape
    depth = ada_w.shape[0]
    mods = _mods(c, ada_w, ada_b).reshape(depth, b, 6, 1, d)
    tables = _rope_tables(s)
    for l in range(depth):
        m = [mods[l, :, k] for k in range(6)]
        i = l // 2
        if l % 2 == 0:
            x = _hybrid_layer(x, m[0:3], norm_mix_g[l], hyb_w_in[i], hyb_q_norm_g[i], hyb_w_uq[i],
                              hyb_kv_norm_g[i], hyb_w_ukv[i], hyb_ret_gn_g[i], hyb_w_out[i], tables)
        else:
            x = _ssd_layer(x, m[0:3], norm_mix_g[l], ssd_w_in[i], ssd_conv_w[i], ssd_conv_b[i],
                           ssd_dt_bias[i], ssd_a_log[i], ssd_d[i], ssd_norm_g[i], ssd_w_out[i])
        x = _ffn_layer(x, m[3:6], norm_ffn_g[l], ffn_w_up[l], ffn_conv_w[l], ffn_conv_b[l],
                       ffn_w_down[l])
    return _final_norm(x, final_norm_g)
```

```python
import functools
import math

import numpy as np
import jax
import jax.numpy as jnp
from jax import lax
from jax.experimental import pallas as pl
from jax.experimental.pallas import tpu as pltpu

F32 = jnp.float32
BF16 = jnp.bfloat16

D_MODEL = 2048
DEPTH = 4
EPS = 1e-6

RET_HEADS = 4
RET_DK = 256
RET_DV = 512
RET_THETA = 10000.0
RET_QK_W = RET_HEADS * RET_DK
RET_V_W = RET_HEADS * RET_DV

MLA_HEADS = 16
MLA_NOPE = 128
MLA_ROPE = 64
MLA_V = 128
MLA_RANK = 512
MLA_THETA = 10000.0
MLA_QK_PAD = 256
MASK_CHUNK = 64

SSD_INNER = 4096
SSD_HEADDIM = 64
SSD_HEADS = 64
SSD_GROUPS = 8
SSD_HPG = SSD_HEADS // SSD_GROUPS
SSD_STATE = 128
SSD_CONV = 4
SSD_CONV_DIM = SSD_INNER + 2 * SSD_GROUPS * SSD_STATE
SSD_GW = SSD_HPG * SSD_HEADDIM

FFN_HIDDEN = 5632
FFN_CONV = 3

LANE = 128
SUBLANE = 8
VMEM_CAP = 56 * 1024 * 1024
NEG = -1e30

TM_PROJ = 1024
TN_PROJ = 1024
TH_FFN = 512
FFN_ROW_CHUNK = 256
FFN_COL_CHUNK = 256
TM_NORM = 512
TN_RES = 512
L_RET = 256
L_SSD = 256
SSD_GROUPS_PER_STEP = 2
TQ_ATTN = 1024
TK_ATTN = 512
TM_UP = 512
TM_CONV = 512


def _pick(n, pref):
    t = pref
    while t > 1 and n % t:
        t //= 2
    return t if n % t == 0 else n


def _params(n_axes, vmem_bytes):
    limit = int(min(VMEM_CAP, max(16 * 1024 * 1024, vmem_bytes * 1.3 + (4 << 20))))
    return pltpu.CompilerParams(dimension_semantics=("arbitrary",) * n_axes,
                                vmem_limit_bytes=limit)


def _silu(v):
    return v * (1.0 / (1.0 + jnp.exp(-v)))


def _dot(a, b):
    return jnp.dot(a, b, preferred_element_type=F32)


def _dot_nt(a, b):
    return lax.dot_general(a, b, (((1,), (1,)), ((), ())), preferred_element_type=F32)


def _shift_rows(g, k, prev8):
    c = g.shape[1]
    r = pltpu.roll(g, k, 0)
    hr = pltpu.roll(prev8, k, 0)
    rows = lax.broadcasted_iota(jnp.int32, (SUBLANE, c), 0)
    top = jnp.where(rows < k, hr, r[0:SUBLANE])
    if g.shape[0] == SUBLANE:
        return top
    return jnp.concatenate([top, r[SUBLANE:]], axis=0)


def _causal_conv(g, w_ref, b_ref, prev8):
    width = w_ref.shape[0]
    y = g * w_ref[width - 1:width, :] + b_ref[...]
    for k in range(1, width):
        y = y + _shift_rows(g, k, prev8) * w_ref[width - 1 - k:width - k, :]
    return y


def _mods_kernel(c_ref, w_ref, b_ref, o_ref):
    h = _silu(c_ref[...]).astype(BF16)
    o_ref[...] = _dot(h, w_ref[...].astype(BF16)) + b_ref[...]


def _mods(c, ada_w, ada_b):
    depth, d, n = ada_w.shape
    b = c.shape[0]
    tn = _pick(n, 1024)
    return pl.pallas_call(
        _mods_kernel,
        grid=(depth, n // tn),
        in_specs=[pl.BlockSpec((b, d), lambda l, j: (0, 0)),
                  pl.BlockSpec((None, d, tn), lambda l, j: (l, 0, j)),
                  pl.BlockSpec((None, 1, tn), lambda l, j: (l, 0, j))],
        out_specs=pl.BlockSpec((None, b, tn), lambda l, j: (l, 0, j)),
        out_shape=jax.ShapeDtypeStruct((depth, b, n), F32),
        compiler_params=_params(2, 2 * d * tn * 4 + d * tn * 2),
        name="adaln_mods",
    )(c, ada_w, ada_b.reshape(depth, 1, n))


def _norm_mod_kernel(x_ref, g_ref, sh_ref, sc_ref, o_ref):
    xf = x_ref[...]
    inv = lax.rsqrt(jnp.mean(xf * xf, axis=-1, keepdims=True) + EPS)
    o_ref[...] = ((xf * inv * g_ref[...]) * (1.0 + sc_ref[...]) + sh_ref[...]).astype(o_ref.dtype)


def _norm_mod(x, g, shift, scale):
    b, s, d = x.shape
    tm = _pick(s, TM_NORM)
    return pl.pallas_call(
        _norm_mod_kernel,
        grid=(b, s // tm),
        in_specs=[pl.BlockSpec((None, tm, d), lambda bi, i: (bi, i, 0)),
                  pl.BlockSpec((1, d), lambda bi, i: (0, 0)),
                  pl.BlockSpec((None, 1, d), lambda bi, i: (bi, 0, 0)),
                  pl.BlockSpec((None, 1, d), lambda bi, i: (bi, 0, 0))],
        out_specs=pl.BlockSpec((None, tm, d), lambda bi, i: (bi, i, 0)),
        out_shape=jax.ShapeDtypeStruct((b, s, d), BF16),
        compiler_params=_params(2, 8 * tm * d * 4),
        name="norm_mod",
    )(x, g.reshape(1, d), shift, scale)


def _matmul_kernel(h_ref, w_ref, o_ref):
    o_ref[...] = _dot(h_ref[...], w_ref[...]).astype(o_ref.dtype)


def _matmul(h, w, col0, n, out_dtype, name):
    b, s, k = h.shape
    tm = _pick(s, TM_PROJ)
    tn = _pick(n, TN_PROJ) if n % TN_PROJ == 0 else n
    assert col0 % tn == 0
    off = col0 // tn
    osz = jnp.dtype(out_dtype).itemsize
    vmem = 2 * tm * k * 2 + 2 * k * tn * 2 + 2 * tm * tn * osz + tm * tn * 4
    return pl.pallas_call(
        _matmul_kernel,
        grid=(b, s // tm, n // tn),
        in_specs=[pl.BlockSpec((None, tm, k), lambda bi, i, j: (bi, i, 0)),
                  pl.BlockSpec((k, tn), lambda bi, i, j: (0, j + off))],
        out_specs=pl.BlockSpec((None, tm, tn), lambda bi, i, j: (bi, i, j)),
        out_shape=jax.ShapeDtypeStruct((b, s, n), out_dtype),
        compiler_params=_params(3, vmem),
        name=name,
    )(h, w)


def _ffn_up_kernel(h_ref, wu_ref, wg_ref, cw_ref, cb_ref, o_ref, halo_scr):
    si = pl.program_id(1)
    j = pl.program_id(2)
    tm, th = o_ref.shape
    rc = _pick(tm, FFN_ROW_CHUNK)
    cc = _pick(th, FFN_COL_CHUNK)
    for c in range(th // cc):
        cols = slice(c * cc, (c + 1) * cc)
        prev8 = jnp.where(si == 0, 0.0, halo_scr[j, :, cols])
        for r in range(tm // rc):
            rows = slice(r * rc, (r + 1) * rc)
            hr = h_ref[rows, :]
            u = _dot(hr, wu_ref[:, cols])
            gate = _dot(hr, wg_ref[:, cols])
            gc = _causal_conv(gate, cw_ref.at[:, cols], cb_ref.at[:, cols], prev8)
            o_ref[rows, cols] = (_silu(gc) * u).astype(o_ref.dtype)
            prev8 = gate[rc - SUBLANE:rc, :]
        halo_scr[j, :, cols] = prev8


def _ffn_up(h, w_up, conv_w, conv_b):
    b, s, d = h.shape
    hid = conv_w.shape[1]
    tm = _pick(s, TM_PROJ)
    th = _pick(hid, TH_FFN)
    nj = hid // th
    vmem = 2 * tm * d * 2 + 4 * d * th * 2 + 2 * tm * th * 2 + 8 * tm * th * 4
    return pl.pallas_call(
        _ffn_up_kernel,
        grid=(b, s // tm, nj),
        in_specs=[pl.BlockSpec((None, tm, d), lambda bi, i, j: (bi, i, 0)),
                  pl.BlockSpec((d, th), lambda bi, i, j: (0, j)),
                  pl.BlockSpec((d, th), lambda bi, i, j: (0, j + nj)),
                  pl.BlockSpec((FFN_CONV, th), lambda bi, i, j: (0, j)),
                  pl.BlockSpec((1, th), lambda bi, i, j: (0, j))],
        out_specs=pl.BlockSpec((None, tm, th), lambda bi, i, j: (bi, i, j)),
        out_shape=jax.ShapeDtypeStruct((b, s, hid), BF16),
        scratch_shapes=[pltpu.VMEM((nj, SUBLANE, th), F32)],
        compiler_params=_params(3, vmem),
        name="ffn_up",
    )(h, w_up, w_up, conv_w, conv_b.reshape(1, hid))


def _matmul_res_kernel(*refs, n_a):
    a_refs = refs[:n_a]
    w_refs = refs[n_a:2 * n_a]
    x_ref, gt_ref, o_ref = refs[2 * n_a:]
    acc = _dot(a_refs[0][...], w_refs[0][...])
    for a_ref, w_ref in zip(a_refs[1:], w_refs[1:]):
        acc = acc + _dot(a_ref[...], w_ref[...])
    o_ref[...] = x_ref[...] + gt_ref[...] * acc


def _matmul_res(a_list, w, x, gate, name):
    b, s, n = x.shape
    ks = [a.shape[2] for a in a_list]
    tm = _pick(s, TM_PROJ)
    tn = _pick(n, TN_RES)
    in_specs = [pl.BlockSpec((None, tm, k), lambda bi, i, j: (bi, i, 0)) for k in ks]
    assert len(set(ks)) == 1
    in_specs += [pl.BlockSpec((ks[0], tn), functools.partial(lambda bi, i, j, p: (p, j), p=p))
                 for p in range(len(ks))]
    in_specs += [pl.BlockSpec((None, tm, tn), lambda bi, i, j: (bi, i, j)),
                 pl.BlockSpec((None, 1, tn), lambda bi, i, j: (bi, 0, j))]
    ktot = sum(ks)
    vmem = 2 * tm * ktot * 2 + 2 * ktot * tn * 2 + 5 * tm * tn * 4
    return pl.pallas_call(
        functools.partial(_matmul_res_kernel, n_a=len(a_list)),
        grid=(b, s // tm, n // tn),
        in_specs=in_specs,
        out_specs=pl.BlockSpec((None, tm, tn), lambda bi, i, j: (bi, i, j)),
        out_shape=jax.ShapeDtypeStruct((b, s, n), F32),
        compiler_params=_params(3, vmem),
        name=name,
    )(*a_list, *([w] * len(a_list)), x, gate)


def _ret_log_gamma(h):
    return math.log1p(-(2.0 ** (-5.0 - h)))


def _retention_kernel(q_ref, k_ref, v_ref, g_ref, cos_ref, sin_ref, gn_ref, o_ref, r_scr):
    ci = pl.program_id(1)
    L = q_ref.shape[0]

    @pl.when(ci == 0)
    def _():
        r_scr[...] = jnp.zeros_like(r_scr)

    cos = cos_ref[...]
    sin = sin_ref[...]
    li = lax.broadcasted_iota(jnp.int32, (L, L), 0)
    si = lax.broadcasted_iota(jnp.int32, (L, L), 1)
    rel = (li - si).astype(F32)
    causal = li >= si
    pos = lax.broadcasted_iota(jnp.int32, (L, 1), 0).astype(F32)
    half = RET_DK // 2

    def rope(t):
        t1 = t[:, :half]
        t2 = t[:, half:]
        return jnp.concatenate([t1 * cos - t2 * sin, t1 * sin + t2 * cos], axis=1)

    for h in range(RET_HEADS):
        lg = _ret_log_gamma(h)
        q = rope(q_ref[:, h * RET_DK:(h + 1) * RET_DK])
        k = rope(k_ref[:, h * RET_DK:(h + 1) * RET_DK]) * (RET_DK ** -0.5)
        v = v_ref[:, h * RET_DV:(h + 1) * RET_DV].astype(F32)
        qb = q.astype(BF16)
        decay = jnp.where(causal, jnp.exp(jnp.where(causal, rel, 0.0) * lg), 0.0)
        inner = _dot_nt(qb, k.astype(BF16)) * decay
        xi = jnp.exp(lg * (pos + 1.0))
        zeta = jnp.exp(lg * (L - 1.0 - pos))
        r_old = r_scr[h]
        y = _dot(inner.astype(BF16), v.astype(BF16)) + _dot(qb, r_old.astype(BF16)) * xi
        kt = jnp.transpose(k).astype(BF16)
        r_scr[h] = math.exp(lg * L) * r_old + _dot(kt, (v * zeta).astype(BF16))
        yc = y - jnp.mean(y, axis=-1, keepdims=True)
        yn = yc * lax.rsqrt(jnp.mean(yc * yc, axis=-1, keepdims=True) + EPS)
        cols = slice(h * RET_DV, (h + 1) * RET_DV)
        gate = _silu(g_ref[:, cols].astype(F32))
        o_ref[:, cols] = (gate * (yn * gn_ref[:, cols])).astype(o_ref.dtype)


def _retention(qk, vg, cos, sin, gn_g):
    b, s, _ = qk.shape
    L = _pick(s, L_RET)
    vmem = 2 * (2 * L * RET_QK_W * 4 + 2 * L * RET_V_W * 2 + L * RET_V_W * 2) \
        + RET_HEADS * RET_DK * RET_DV * 4 + 24 * L * RET_DV * 4
    return pl.pallas_call(
        _retention_kernel,
        grid=(b, s // L),
        in_specs=[pl.BlockSpec((None, L, RET_QK_W), lambda bi, c: (bi, c, 0)),
                  pl.BlockSpec((None, L, RET_QK_W), lambda bi, c: (bi, c, 1)),
                  pl.BlockSpec((None, L, RET_V_W), lambda bi, c: (bi, c, 0)),
                  pl.BlockSpec((None, L, RET_V_W), lambda bi, c: (bi, c, 1)),
                  pl.BlockSpec((L, RET_DK // 2), lambda bi, c: (c, 0)),
                  pl.BlockSpec((L, RET_DK // 2), lambda bi, c: (c, 0)),
                  pl.BlockSpec((1, RET_V_W), lambda bi, c: (0, 0))],
        out_specs=pl.BlockSpec((None, L, RET_V_W), lambda bi, c: (bi, c, 0)),
        out_shape=jax.ShapeDtypeStruct((b, s, RET_V_W), BF16),
        scratch_shapes=[pltpu.VMEM((RET_HEADS, RET_DK, RET_DV), F32)],
        compiler_params=_params(2, vmem),
        name="retention",
    )(qk, qk, vg, vg, cos, sin, gn_g.reshape(1, RET_V_W))


def _rms_rows(x, g):
    return x * lax.rsqrt(jnp.mean(x * x, axis=-1, keepdims=True) + EPS) * g


def _rope_slot(t, cos_t, sin_t):
    lane = lax.broadcasted_iota(jnp.int32, t.shape, 1)
    hr = MLA_ROPE // 2
    swapped = jnp.where(lane < hr, pltpu.roll(t, LANE - hr, 1), pltpu.roll(t, hr, 1))
    return t * cos_t + swapped * sin_t


def _q_up_kernel(c_ref, g_ref, w_ref, cos_ref, sin_ref, o_ref):
    h = _rms_rows(c_ref[...], g_ref[...]).astype(BF16)
    q = _dot(h, w_ref[...]) * MLA_Q_SCALE
    cos_t = cos_ref[...]
    sin_t = sin_ref[...]
    for hd in range(MLA_HEADS):
        base = hd * MLA_QK_PAD
        o_ref[:, base:base + MLA_NOPE] = q[:, base:base + MLA_NOPE].astype(o_ref.dtype)
        rs = q[:, base + MLA_NOPE:base + MLA_QK_PAD]
        o_ref[:, base + MLA_NOPE:base + MLA_QK_PAD] = _rope_slot(rs, cos_t, sin_t).astype(o_ref.dtype)


def _kv_up_kernel(c_ref, kr_ref, g_ref, wk_ref, wv_ref, cos_ref, sin_ref, k_ref, v_ref):
    h = _rms_rows(c_ref[...], g_ref[...]).astype(BF16)
    kn = _dot(h, wk_ref[...])
    v_ref[...] = _dot(h, wv_ref[...]).astype(v_ref.dtype)
    kr = _rope_slot(kr_ref[...], cos_ref[...], sin_ref[...]).astype(k_ref.dtype)
    for hd in range(MLA_HEADS):
        base = hd * MLA_QK_PAD
        k_ref[:, base:base + MLA_NOPE] = kn[:, hd * MLA_NOPE:(hd + 1) * MLA_NOPE].astype(k_ref.dtype)
        k_ref[:, base + MLA_NOPE:base + MLA_QK_PAD] = kr


def _mla_up(small, q_g, kv_g, wq, wk, wv, cos_t, sin_t):
    b, s, _ = small.shape
    tm = _pick(s, TM_UP)
    nq = MLA_HEADS * MLA_QK_PAD
    nv = MLA_HEADS * MLA_V
    tab = pl.BlockSpec((tm, LANE), lambda bi, i: (i, 0))
    q = pl.pallas_call(
        _q_up_kernel,
        grid=(b, s // tm),
        in_specs=[pl.BlockSpec((None, tm, MLA_RANK), lambda bi, i: (bi, i, 0)),
                  pl.BlockSpec((1, MLA_RANK), lambda bi, i: (0, 0)),
                  pl.BlockSpec((MLA_RANK, nq), lambda bi, i: (0, 0)),
                  tab, tab],
        out_specs=pl.BlockSpec((None, tm, nq), lambda bi, i: (bi, i, 0)),
        out_shape=jax.ShapeDtypeStruct((b, s, nq), BF16),
        compiler_params=_params(2, 2 * MLA_RANK * nq * 2 + 2 * tm * nq * 2 + 3 * tm * nq * 4),
        name="mla_q_up",
    )(small, q_g.reshape(1, MLA_RANK), wq, cos_t, sin_t)
    k, v = pl.pallas_call(
        _kv_up_kernel,
        grid=(b, s // tm),
        in_specs=[pl.BlockSpec((None, tm, MLA_RANK), lambda bi, i: (bi, i, 1)),
                  pl.BlockSpec((None, tm, LANE), lambda bi, i: (bi, i, 2 * MLA_RANK // LANE)),
                  pl.BlockSpec((1, MLA_RANK), lambda bi, i: (0, 0)),
                  pl.BlockSpec((MLA_RANK, nv), lambda bi, i: (0, 0)),
                  pl.BlockSpec((MLA_RANK, nv), lambda bi, i: (0, 0)),
                  tab, tab],
        out_specs=[pl.BlockSpec((None, tm, nq), lambda bi, i: (bi, i, 0)),
                   pl.BlockSpec((None, tm, nv), lambda bi, i: (bi, i, 0))],
        out_shape=[jax.ShapeDtypeStruct((b, s, nq), BF16),
                   jax.ShapeDtypeStruct((b, s, nv), BF16)],
        compiler_params=_params(2, 4 * MLA_RANK * nv * 2 + 2 * tm * (nq + nv) * 2 + 3 * tm * nq * 4),
        name="mla_kv_up",
    )(small, small, kv_g.reshape(1, MLA_RANK), wk, wv, cos_t, sin_t)
    return q, k, v


MLA_Q_SCALE = ((MLA_NOPE + MLA_ROPE) ** -0.5) * math.log2(math.e)


def _mla_attn_kernel(q_ref, k_ref, v_ref, o_ref, *, tk):
    qi = pl.program_id(2)
    tq = q_ref.shape[0]
    q = q_ref[...]

    def update(carry, s2, v):
        m, l, acc = carry
        m_new = jnp.maximum(m, jnp.max(s2, axis=-1, keepdims=True))
        p = jnp.exp2(s2 - m_new)
        alpha = jnp.exp2(m - m_new)
        l = alpha * l + jnp.sum(p, axis=-1, keepdims=True)
        acc = alpha * acc + _dot(p.astype(BF16), v)
        return m_new, l, acc

    def body(ki, carry):
        rows = pl.ds(pl.multiple_of(ki * tk, tk), tk)
        return update(carry, _dot_nt(q, k_ref[rows, :]), v_ref[rows, :])

    init = (jnp.full((tq, 1), NEG, F32), jnp.zeros((tq, 1), F32), jnp.zeros((tq, MLA_V), F32))
    carry = lax.fori_loop(0, qi * (tq // tk), body, init)

    qc = lax.broadcasted_iota(jnp.int32, (tq, tk), 0) // MASK_CHUNK
    kc = lax.broadcasted_iota(jnp.int32, (tq, tk), 1) // MASK_CHUNK
    for j in range(tq // tk):
        rows = pl.ds(pl.multiple_of(qi * tq + j * tk, tk), tk)
        s2 = jnp.where(kc + (j * tk) // MASK_CHUNK <= qc, _dot_nt(q, k_ref[rows, :]), NEG)
        carry = update(carry, s2, v_ref[rows, :])
    m, l, acc = carry
    o_ref[...] = (acc / l).astype(o_ref.dtype)


def _mla_attention(q, k, v):
    b, s, _ = q.shape
    tq = _pick(s, TQ_ATTN)
    tk = _pick(tq, TK_ATTN)
    vmem = 2 * (tq * MLA_QK_PAD * 2 + s * MLA_QK_PAD * 2 + s * MLA_V * 2 + tq * MLA_V * 2) \
        + 6 * tq * tk * 4
    return pl.pallas_call(
        functools.partial(_mla_attn_kernel, tk=tk),
        grid=(b, MLA_HEADS, s // tq),
        in_specs=[pl.BlockSpec((None, tq, MLA_QK_PAD), lambda bi, h, i: (bi, i, h)),
                  pl.BlockSpec((None, s, MLA_QK_PAD), lambda bi, h, i: (bi, 0, h)),
                  pl.BlockSpec((None, s, MLA_V), lambda bi, h, i: (bi, 0, h))],
        out_specs=pl.BlockSpec((None, tq, MLA_V), lambda bi, h, i: (bi, i, h)),
        out_shape=jax.ShapeDtypeStruct((b, s, MLA_HEADS * MLA_V), BF16),
        compiler_params=_params(3, vmem),
        name="mla_attention",
    )(q, k, v)


def _ssd_prep_kernel(x_ref, cw_ref, cb_ref, o_ref, halo_scr):
    si = pl.program_id(1)
    j = pl.program_id(2)
    tm = x_ref.shape[0]
    width = cw_ref.shape[0]
    body = tm - SUBLANE
    acc = x_ref[SUBLANE:tm, :] * cw_ref[width - 1:width, :] + cb_ref[...]
    for k in range(1, width):
        acc = acc + x_ref[pl.ds(SUBLANE - k, body), :] * cw_ref[width - 1 - k:width - k, :]
    o_ref[SUBLANE:tm, :] = _silu(acc)
    prev8 = jnp.where(si == 0, 0.0, halo_scr[j])
    o_ref[0:SUBLANE, :] = _silu(_causal_conv(x_ref[0:SUBLANE, :], cw_ref, cb_ref, prev8))
    halo_scr[j] = x_ref[tm - SUBLANE:tm, :]


def _ssd_prep(xbc, conv_w, conv_b):
    b, s, n = xbc.shape
    tm = _pick(s, TM_CONV)
    tc = _pick(n, 2048)
    nj = n // tc
    return pl.pallas_call(
        _ssd_prep_kernel,
        grid=(b, s // tm, nj),
        in_specs=[pl.BlockSpec((None, tm, tc), lambda bi, i, j: (bi, i, j)),
                  pl.BlockSpec((SSD_CONV, tc), lambda bi, i, j: (0, j)),
                  pl.BlockSpec((1, tc), lambda bi, i, j: (0, j))],
        out_specs=pl.BlockSpec((None, tm, tc), lambda bi, i, j: (bi, i, j)),
        out_shape=jax.ShapeDtypeStruct((b, s, n), F32),
        scratch_shapes=[pltpu.VMEM((nj, SUBLANE, tc), F32)],
        compiler_params=_params(3, 8 * tm * tc * 4),
        name="ssd_conv",
    )(xbc, conv_w, conv_b.reshape(1, n))


def _softplus_kernel(x_ref, b_ref, o_ref):
    v = x_ref[...] + b_ref[...]
    o_ref[...] = jnp.maximum(v, 0.0) + jnp.log1p(jnp.exp(-jnp.abs(v)))


def _ssd_dt(dt_raw, dt_bias_pad):
    b, s, n = dt_raw.shape
    tm = _pick(s, 1024)
    return pl.pallas_call(
        _softplus_kernel,
        grid=(b, s // tm),
        in_specs=[pl.BlockSpec((None, tm, n), lambda bi, i: (bi, i, 0)),
                  pl.BlockSpec((1, n), lambda bi, i: (0, 0))],
        out_specs=pl.BlockSpec((None, tm, n), lambda bi, i: (bi, i, 0)),
        out_shape=jax.ShapeDtypeStruct((b, s, n), F32),
        compiler_params=_params(2, 8 * tm * n * 4),
        name="ssd_dt",
    )(dt_raw, dt_bias_pad.reshape(1, n))


def _split3(a):
    hi = a.astype(BF16)
    r1 = a - hi.astype(F32)
    mid = r1.astype(BF16)
    lo = (r1 - mid.astype(F32)).astype(BF16)
    return hi, mid, lo


SSD_PIECES = 3
SSD_QW = SSD_PIECES * SSD_HPG
SSD_NCOPY = 3 * SSD_PIECES


def _ssd_expand_matrix():
    r = np.arange(LANE)[:, None]
    c = np.arange(3 * SSD_GW)[None, :]
    hit = (r < 3 * SSD_QW) & (c // SSD_GW == r // SSD_QW) & ((c % SSD_GW) // SSD_HEADDIM == r % SSD_HPG)
    return jnp.asarray(hit, dtype=BF16)


def _ssd_scan_kernel(x_ref, b_ref, c_ref, dtc_ref, dtr_ref, ac_ref, ar_ref, dsk_ref, ex_ref,
                     o_ref, st_scr):
    ci = pl.program_id(2)
    L = x_ref.shape[0]

    @pl.when(ci == 0)
    def _():
        st_scr[...] = jnp.zeros_like(st_scr)

    li = lax.broadcasted_iota(jnp.int32, (L, L), 0)
    si = lax.broadcasted_iota(jnp.int32, (L, L), 1)
    causal = li >= si
    tril = jnp.where(causal, 1.0, 0.0).astype(BF16)
    triu = jnp.where(li <= si, 1.0, 0.0).astype(BF16)
    pad = jnp.zeros((L, LANE - SSD_NCOPY * SSD_HPG), F32)
    lane = lax.broadcasted_iota(jnp.int32, (L, LANE), 1)
    piece = (lane % SSD_QW) // SSD_HPG
    left = lane < SSD_HEADDIM

    for gi in range(dtc_ref.shape[0]):
        gcols = slice(gi * SSD_GW, (gi + 1) * SSD_GW)
        ncols = slice(gi * SSD_STATE, (gi + 1) * SSD_STATE)

        dt_c = jnp.concatenate([dtc_ref[gi], pad], axis=1)
        a_c = jnp.concatenate([-jnp.exp(ac_ref[gi]), pad[0:1]], axis=1)
        acum = sum(_dot(tril, part) for part in _split3(dt_c * a_c))
        tot = acum[L - 1:L, :]
        quantity = jnp.where(lane < SSD_QW, dt_c,
                             jnp.where(lane < 2 * SSD_QW, jnp.exp(acum), jnp.exp(tot - acum)))
        hi, mid, lo = (p.astype(F32) for p in _split3(quantity))
        cols = jnp.where(piece == 0, hi, jnp.where(piece == 1, mid, lo)).astype(BF16)
        expanded = _dot(cols, ex_ref[...])
        dt_x = expanded[:, 0:SSD_GW]
        ea_x = expanded[:, SSD_GW:2 * SSD_GW]
        te_x = expanded[:, 2 * SSD_GW:3 * SSD_GW]

        acum_r = sum(_dot(part, triu) for part in _split3(dtr_ref[gi] * -jnp.exp(ar_ref[gi])))

        xs = x_ref[:, gcols]
        bm = b_ref[:, ncols]
        cm = c_ref[:, ncols].astype(BF16)
        cb = _dot_nt(cm, bm.astype(BF16))
        state = st_scr[gi]
        cs = _dot(cm, state.astype(BF16))
        bmt = jnp.transpose(bm).astype(BF16)
        xdt = xs * dt_x

        ys = []
        for t in range(SSD_GW // LANE):
            x2 = xdt[:, t * LANE:(t + 1) * LANE]
            y_t = None
            for h, keep in ((2 * t, left), (2 * t + 1, jnp.logical_not(left))):
                seg = acum[:, h:h + 1] - acum_r[h:h + 1, :]
                lmat = jnp.exp(jnp.where(causal, seg, NEG))
                part = _dot((cb * lmat).astype(BF16), jnp.where(keep, x2, 0.0).astype(BF16))
                y_t = part if y_t is None else y_t + part
            ys.append(y_t)
        y_diag = jnp.concatenate(ys, axis=1)
        o_ref[:, gcols] = y_diag + cs * ea_x + xs * dsk_ref[:, gcols]
        st_scr[gi] = state * ea_x[L - 1:L, :] + _dot(bmt, (xdt * te_x).astype(BF16))


def _ssd_scan(xbc, dtp, a_log, d_skip):
    b, s, _ = xbc.shape
    L = _pick(s, L_SSD)
    g = SSD_GROUPS
    gps = SSD_GROUPS_PER_STEP
    dt_g = dtp.reshape(b, s, g, SSD_HPG)
    dtc = jnp.tile(jnp.transpose(dt_g, (0, 2, 1, 3)), (1, 1, 1, SSD_NCOPY))
    dtr = jnp.transpose(dt_g, (0, 2, 3, 1))
    a_col = jnp.tile(a_log.reshape(g, 1, SSD_HPG), (1, 1, SSD_NCOPY))
    a_row = a_log.reshape(g, SSD_HPG, 1)
    ncol = SSD_NCOPY * SSD_HPG
    dsk = jnp.repeat(d_skip, SSD_HEADDIM).reshape(1, SSD_INNER)
    xw = gps * SSD_GW
    nw = gps * SSD_STATE
    boff = SSD_INNER // nw
    coff = boff + g // gps
    vmem = gps * (4 * L * SSD_GW * 4 + 24 * L * L * 4 + 16 * L * SSD_GW * 4)
    return pl.pallas_call(
        _ssd_scan_kernel,
        grid=(b, g // gps, s // L),
        in_specs=[pl.BlockSpec((None, L, xw), lambda bi, gi, c: (bi, c, gi)),
                  pl.BlockSpec((None, L, nw), lambda bi, gi, c: (bi, c, boff + gi)),
                  pl.BlockSpec((None, L, nw), lambda bi, gi, c: (bi, c, coff + gi)),
                  pl.BlockSpec((None, gps, L, ncol), lambda bi, gi, c: (bi, gi, c, 0)),
                  pl.BlockSpec((None, gps, SSD_HPG, L), lambda bi, gi, c: (bi, gi, 0, c)),
                  pl.BlockSpec((gps, 1, ncol), lambda bi, gi, c: (gi, 0, 0)),
                  pl.BlockSpec((gps, SSD_HPG, 1), lambda bi, gi, c: (gi, 0, 0)),
                  pl.BlockSpec((1, xw), lambda bi, gi, c: (0, gi)),
                  pl.BlockSpec((LANE, 3 * SSD_GW), lambda bi, gi, c: (0, 0))],
        out_specs=pl.BlockSpec((None, L, xw), lambda bi, gi, c: (bi, c, gi)),
        out_shape=jax.ShapeDtypeStruct((b, s, SSD_INNER), F32),
        scratch_shapes=[pltpu.VMEM((gps, SSD_STATE, SSD_GW), F32)],
        compiler_params=_params(3, vmem),
        name="ssd_scan",
    )(xbc, xbc, xbc, dtc, dtr, a_col, a_row, dsk, _ssd_expand_matrix())


def _gate_norm_kernel(y_ref, z_ref, g_ref, o_ref):
    v = y_ref[...] * _silu(z_ref[...].astype(F32))
    o_ref[...] = _rms_rows(v, g_ref[...]).astype(o_ref.dtype)


def _gate_norm(y, z, g):
    b, s, n = y.shape
    tm = _pick(s, 256)
    return pl.pallas_call(
        _gate_norm_kernel,
        grid=(b, s // tm),
        in_specs=[pl.BlockSpec((None, tm, n), lambda bi, i: (bi, i, 0)),
                  pl.BlockSpec((None, tm, n), lambda bi, i: (bi, i, 0)),
                  pl.BlockSpec((1, n), lambda bi, i: (0, 0))],
        out_specs=pl.BlockSpec((None, tm, n), lambda bi, i: (bi, i, 0)),
        out_shape=jax.ShapeDtypeStruct((b, s, n), BF16),
        compiler_params=_params(2, 8 * tm * n * 4),
        name="ssd_gate_norm",
    )(y, z, g.reshape(1, n))


def _final_norm_kernel(x_ref, g_ref, o_ref):
    o_ref[...] = _rms_rows(x_ref[...], g_ref[...])


def _final_norm(x, g):
    b, s, n = x.shape
    tm = _pick(s, 512)
    return pl.pallas_call(
        _final_norm_kernel,
        grid=(b, s // tm),
        in_specs=[pl.BlockSpec((None, tm, n), lambda bi, i: (bi, i, 0)),
                  pl.BlockSpec((1, n), lambda bi, i: (0, 0))],
        out_specs=pl.BlockSpec((None, tm, n), lambda bi, i: (bi, i, 0)),
        out_shape=jax.ShapeDtypeStruct((b, s, n), F32),
        compiler_params=_params(2, 8 * tm * n * 4),
        name="final_norm",
    )(x, g.reshape(1, n))


def _rope_tables(s):
    pos = jnp.arange(s, dtype=jnp.int32).astype(F32)[:, None]
    half = RET_DK // 2
    inv = RET_THETA ** (-jnp.arange(half, dtype=F32) / half)
    ang = pos * inv[None, :]
    ret_cos, ret_sin = jnp.cos(ang), jnp.sin(ang)
    hr = MLA_ROPE // 2
    inv_m = MLA_THETA ** (-jnp.arange(hr, dtype=F32) / hr)
    ang_m = pos * inv_m[None, :]
    cm, sm = jnp.cos(ang_m), jnp.sin(ang_m)
    pad = LANE - MLA_ROPE
    mla_cos = jnp.concatenate([cm, cm, jnp.ones((s, pad), F32)], axis=1)
    mla_sin = jnp.concatenate([-sm, sm, jnp.zeros((s, pad), F32)], axis=1)
    return ret_cos, ret_sin, mla_cos, mla_sin


def _hybrid_layer(x, mods, norm_g, w_in, q_g, w_uq, kv_g, w_ukv, gn_g, w_out, tables):
    sh, sc, gt = mods
    ret_cos, ret_sin, mla_cos, mla_sin = tables
    w_in = w_in.astype(BF16)
    o_v = 2 * RET_QK_W
    o_c = o_v + 2 * RET_V_W
    o_kr = o_c + 2 * MLA_RANK
    w_small = jnp.concatenate(
        [w_in[:, o_c:o_kr + MLA_ROPE], jnp.zeros((D_MODEL, LANE - MLA_ROPE), BF16)], axis=1)
    h = _norm_mod(x, norm_g, sh, sc)
    qk = _matmul(h, w_in, 0, o_v, F32, "hyb_in_qk")
    vg = _matmul(h, w_in, o_v, o_c - o_v, BF16, "hyb_in_vg")
    small = _matmul(h, w_small, 0, w_small.shape[1], F32, "hyb_in_latent")
    y_ret = _retention(qk, vg, ret_cos, ret_sin, gn_g)

    wq = w_uq.astype(BF16).reshape(MLA_RANK, MLA_HEADS, MLA_NOPE + MLA_ROPE)
    wq = jnp.pad(wq, ((0, 0), (0, 0), (0, MLA_QK_PAD - MLA_NOPE - MLA_ROPE)))
    wq = wq.reshape(MLA_RANK, MLA_HEADS * MLA_QK_PAD)
    wkv = w_ukv.astype(BF16).reshape(MLA_RANK, MLA_HEADS, MLA_NOPE + MLA_V)
    wk = wkv[:, :, :MLA_NOPE].reshape(MLA_RANK, MLA_HEADS * MLA_NOPE)
    wv = wkv[:, :, MLA_NOPE:].reshape(MLA_RANK, MLA_HEADS * MLA_V)
    q, k, v = _mla_up(small, q_g, kv_g, wq, wk, wv, mla_cos, mla_sin)
    y_mla = _mla_attention(q, k, v)
    return _matmul_res([y_ret, y_mla], w_out.astype(BF16), x, gt, "hyb_out")


def _ssd_layer(x, mods, norm_g, w_in, conv_w, conv_b, dt_bias, a_log, d_skip, ssd_g, w_out):
    sh, sc, gt = mods
    w_in = w_in.astype(BF16)
    o_x = SSD_INNER
    o_dt = o_x + SSD_CONV_DIM
    w_dt = jnp.concatenate(
        [w_in[:, o_dt:], jnp.zeros((D_MODEL, LANE - SSD_HEADS), BF16)], axis=1)
    h = _norm_mod(x, norm_g, sh, sc)
    z = _matmul(h, w_in, 0, o_x, BF16, "ssd_in_z")
    xbc = _matmul(h, w_in, o_x, o_dt - o_x, F32, "ssd_in_xbc")
    dt_raw = _matmul(h, w_dt, 0, LANE, F32, "ssd_in_dt")
    xbc = _ssd_prep(xbc, conv_w, conv_b)
    dtp = _ssd_dt(dt_raw, jnp.pad(dt_bias, (0, LANE - SSD_HEADS)))[:, :, :SSD_HEADS]
    y = _ssd_scan(xbc, dtp, a_log, d_skip)
    hn = _gate_norm(y, z, ssd_g)
    return _matmul_res([hn], w_out.astype(BF16), x, gt, "ssd_out")


def _ffn_layer(x, mods, norm_g, w_up, conv_w, conv_b, w_down):
    sh, sc, gt = mods
    a = _ffn_up(_norm_mod(x, norm_g, sh, sc), w_up.astype(BF16), conv_w, conv_b)
    return _matmul_res([a], w_down.astype(BF16), x, gt, "ffn_down")


def kernel(x, c, ada_w, ada_b, norm_mix_g, norm_ffn_g, hyb_w_in, hyb_q_norm_g, hyb_w_uq, hyb_kv_norm_g, hyb_w_ukv, hyb_ret_gn_g, hyb_w_out, ssd_w_in, ssd_conv_w, ssd_conv_b, ssd_dt_bias, ssd_a_log, ssd_d, ssd_norm_g, ssd_w_out, ffn_w_up, ffn_conv_w, ffn_conv_b, ffn_w_down, final_norm_g):
    b, s, d = x.shape
    depth = ada_w.shape[0]
    mods = _mods(c, ada_w, ada_b).reshape(depth, b, 6, 1, d)
    tables = _rope_tables(s)
    for l in range(depth):
        m = [mods[l, :, k] for k in range(6)]
        i = l // 2
        if l % 2 == 0:
            x = _hybrid_layer(x, m[0:3], norm_mix_g[l], hyb_w_in[i], hyb_q_norm_g[i], hyb_w_uq[i],
                              hyb_kv_norm_g[i], hyb_w_ukv[i], hyb_ret_gn_g[i], hyb_w_out[i], tables)
        else:
            x = _ssd_layer(x, m[0:3], norm_mix_g[l], ssd_w_in[i], ssd_conv_w[i], ssd_conv_b[i],
                           ssd_dt_bias[i], ssd_a_log[i], ssd_d[i], ssd_norm_g[i], ssd_w_out[i])
        x = _ffn_layer(x, m[3:6], norm_ffn_g[l], ffn_w_up[l], ffn_conv_w[l], ffn_conv_b[l],
                       ffn_w_down[l])
    return _final_norm(x, final_norm_g)
```

```python
import functools
import math

import numpy as np
import jax
import jax.numpy as jnp
from jax import lax
from jax.experimental import pallas as pl
from jax.experimental.pallas import tpu as pltpu

F32 = jnp.float32
BF16 = jnp.bfloat16

D_MODEL = 2048
DEPTH = 4
EPS = 1e-6

RET_HEADS = 4
RET_DK = 256
RET_DV = 512
RET_THETA = 10000.0
RET_QK_W = RET_HEADS * RET_DK
RET_V_W = RET_HEADS * RET_DV

MLA_HEADS = 16
MLA_NOPE = 128
MLA_ROPE = 64
MLA_V = 128
MLA_RANK = 512
MLA_THETA = 10000.0
MLA_QK_PAD = 256
MASK_CHUNK = 64

SSD_INNER = 4096
SSD_HEADDIM = 64
SSD_HEADS = 64
SSD_GROUPS = 8
SSD_HPG = SSD_HEADS // SSD_GROUPS
SSD_STATE = 128
SSD_CONV = 4
SSD_CONV_DIM = SSD_INNER + 2 * SSD_GROUPS * SSD_STATE
SSD_GW = SSD_HPG * SSD_HEADDIM

FFN_HIDDEN = 5632
FFN_CONV = 3

LANE = 128
SUBLANE = 8
VMEM_CAP = 56 * 1024 * 1024
NEG = -1e30

TM_PROJ = 1024
TM_MATMUL = 2048
TN_PROJ = 1024
TH_FFN = 512
FFN_ROW_CHUNK = 256
FFN_COL_CHUNK = 256
TM_NORM = 1024
NORM_ROW_CHUNK = 16
TN_RES = 512
L_RET = 256
L_SSD = 256
SSD_GROUPS_PER_STEP = 2
TQ_ATTN = 1024
TK_ATTN = 1024
TM_UP = 512


def _pick(n, pref):
    t = pref
    while t > 1 and n % t:
        t //= 2
    return t if n % t == 0 else n


def _params(n_axes, vmem_bytes):
    limit = int(min(VMEM_CAP, max(16 * 1024 * 1024, vmem_bytes * 1.3 + (4 << 20))))
    return pltpu.CompilerParams(dimension_semantics=("arbitrary",) * n_axes,
                                vmem_limit_bytes=limit)


def _silu(v):
    return v * (1.0 / (1.0 + jnp.exp(-v)))


def _dot(a, b):
    return jnp.dot(a, b, preferred_element_type=F32)


def _dot_nt(a, b):
    return lax.dot_general(a, b, (((1,), (1,)), ((), ())), preferred_element_type=F32)


def _shift_rows(g, k, prev8):
    c = g.shape[1]
    r = pltpu.roll(g, k, 0)
    hr = pltpu.roll(prev8, k, 0)
    rows = lax.broadcasted_iota(jnp.int32, (SUBLANE, c), 0)
    top = jnp.where(rows < k, hr, r[0:SUBLANE])
    if g.shape[0] == SUBLANE:
        return top
    return jnp.concatenate([top, r[SUBLANE:]], axis=0)


def _causal_conv(g, w_ref, b_ref, prev8):
    width = w_ref.shape[0]
    y = g * w_ref[width - 1:width, :] + b_ref[...]
    for k in range(1, width):
        y = y + _shift_rows(g, k, prev8) * w_ref[width - 1 - k:width - k, :]
    return y


def _mods_kernel(c_ref, w_ref, b_ref, o_ref):
    h = _silu(c_ref[...]).astype(BF16)
    o_ref[...] = _dot(h, w_ref[...].astype(BF16)) + b_ref[...]


def _mods(c, ada_w, ada_b):
    depth, d, n = ada_w.shape
    b = c.shape[0]
    tn = _pick(n, 1024)
    return pl.pallas_call(
        _mods_kernel,
        grid=(depth, n // tn),
        in_specs=[pl.BlockSpec((b, d), lambda l, j: (0, 0)),
                  pl.BlockSpec((None, d, tn), lambda l, j: (l, 0, j)),
                  pl.BlockSpec((None, 1, tn), lambda l, j: (l, 0, j))],
        out_specs=pl.BlockSpec((None, b, tn), lambda l, j: (l, 0, j)),
        out_shape=jax.ShapeDtypeStruct((depth, b, n), F32),
        compiler_params=_params(2, 2 * d * tn * 4 + d * tn * 2),
        name="adaln_mods",
    )(c, ada_w, ada_b.reshape(depth, 1, n))


def _norm_mod_kernel(x_ref, g_ref, sh_ref, sc_ref, o_ref):
    rc = NORM_ROW_CHUNK
    gs = g_ref[...] * (1.0 + sc_ref[...])
    sh = sh_ref[...]

    def body(r, carry):
        rows = pl.ds(pl.multiple_of(r * rc, rc), rc)
        xf = x_ref[rows, :]
        inv = lax.rsqrt(jnp.mean(xf * xf, axis=-1, keepdims=True) + EPS)
        o_ref[rows, :] = ((xf * inv) * gs + sh).astype(o_ref.dtype)
        return carry

    lax.fori_loop(0, x_ref.shape[0] // rc, body, 0, unroll=4)


def _norm_mod(x, g, shift, scale):
    b, s, d = x.shape
    tm = _pick(s, TM_NORM)
    return pl.pallas_call(
        _norm_mod_kernel,
        grid=(b, s // tm),
        in_specs=[pl.BlockSpec((None, tm, d), lambda bi, i: (bi, i, 0)),
                  pl.BlockSpec((1, d), lambda bi, i: (0, 0)),
                  pl.BlockSpec((None, 1, d), lambda bi, i: (bi, 0, 0)),
                  pl.BlockSpec((None, 1, d), lambda bi, i: (bi, 0, 0))],
        out_specs=pl.BlockSpec((None, tm, d), lambda bi, i: (bi, i, 0)),
        out_shape=jax.ShapeDtypeStruct((b, s, d), BF16),
        compiler_params=_params(2, 8 * tm * d * 4),
        name="norm_mod",
    )(x, g.reshape(1, d), shift, scale)


def _matmul_kernel(h_ref, w_ref, o_ref):
    o_ref[...] = _dot(h_ref[...], w_ref[...]).astype(o_ref.dtype)


def _w_spec(w, rows, cols, index):
    if w.ndim == 2:
        return pl.BlockSpec((rows, cols), lambda bi, i, j: index(bi, i, j)[1:])
    return pl.BlockSpec((None, rows, cols), index)


def _matmul(h, w, layer, col0, n, out_dtype, name):
    b, s, k = h.shape
    tm = _pick(s, TM_MATMUL)
    tn = _pick(n, TN_PROJ) if n % TN_PROJ == 0 else n
    assert col0 % tn == 0
    off = col0 // tn
    osz = jnp.dtype(out_dtype).itemsize
    vmem = 2 * tm * k * 2 + 2 * k * tn * 2 + 2 * tm * tn * osz + tm * tn * 4
    return pl.pallas_call(
        _matmul_kernel,
        grid=(b, s // tm, n // tn),
        in_specs=[pl.BlockSpec((None, tm, k), lambda bi, i, j: (bi, i, 0)),
                  _w_spec(w, k, tn, lambda bi, i, j: (layer, 0, j + off))],
        out_specs=pl.BlockSpec((None, tm, tn), lambda bi, i, j: (bi, i, j)),
        out_shape=jax.ShapeDtypeStruct((b, s, n), out_dtype),
        compiler_params=_params(3, vmem),
        name=name,
    )(h, w)


def _ffn_up_kernel(h_ref, wu_ref, wg_ref, cw_ref, cb_ref, o_ref, halo_scr):
    si = pl.program_id(1)
    j = pl.program_id(2)
    tm, th = o_ref.shape
    rc = _pick(tm, FFN_ROW_CHUNK)
    cc = _pick(th, FFN_COL_CHUNK)
    for c in range(th // cc):
        cols = slice(c * cc, (c + 1) * cc)
        prev8 = jnp.where(si == 0, 0.0, halo_scr[j, :, cols])
        for r in range(tm // rc):
            rows = slice(r * rc, (r + 1) * rc)
            hr = h_ref[rows, :]
            u = _dot(hr, wu_ref[:, cols])
            gate = _dot(hr, wg_ref[:, cols])
            gc = _causal_conv(gate, cw_ref.at[:, cols], cb_ref.at[:, cols], prev8)
            o_ref[rows, cols] = (_silu(gc) * u).astype(o_ref.dtype)
            prev8 = gate[rc - SUBLANE:rc, :]
        halo_scr[j, :, cols] = prev8


def _ffn_up(h, w_up, layer, conv_w, conv_b):
    b, s, d = h.shape
    hid = conv_w.shape[1]
    tm = _pick(s, TM_MATMUL)
    th = _pick(hid, TH_FFN)
    nj = hid // th
    vmem = 2 * tm * d * 2 + 4 * d * th * 2 + 2 * tm * th * 2 + 8 * tm * th * 4
    return pl.pallas_call(
        _ffn_up_kernel,
        grid=(b, s // tm, nj),
        in_specs=[pl.BlockSpec((None, tm, d), lambda bi, i, j: (bi, i, 0)),
                  _w_spec(w_up, d, th, lambda bi, i, j: (layer, 0, j)),
                  _w_spec(w_up, d, th, lambda bi, i, j: (layer, 0, j + nj)),
                  pl.BlockSpec((FFN_CONV, th), lambda bi, i, j: (0, j)),
                  pl.BlockSpec((1, th), lambda bi, i, j: (0, j))],
        out_specs=pl.BlockSpec((None, tm, th), lambda bi, i, j: (bi, i, j)),
        out_shape=jax.ShapeDtypeStruct((b, s, hid), BF16),
        scratch_shapes=[pltpu.VMEM((nj, SUBLANE, th), F32)],
        compiler_params=_params(3, vmem),
        name="ffn_up",
    )(h, w_up, w_up, conv_w, conv_b.reshape(1, hid))


def _matmul_res_kernel(*refs, n_a):
    a_refs = refs[:n_a]
    w_refs = refs[n_a:2 * n_a]
    x_ref, gt_ref, o_ref = refs[2 * n_a:]
    acc = _dot(a_refs[0][...], w_refs[0][...])
    for a_ref, w_ref in zip(a_refs[1:], w_refs[1:]):
        acc = acc + _dot(a_ref[...], w_ref[...])
    o_ref[...] = x_ref[...] + gt_ref[...] * acc


def _matmul_res(a_list, w, layer, x, gate, name):
    b, s, n = x.shape
    ks = [a.shape[2] for a in a_list]
    tm = _pick(s, TM_PROJ)
    tn = _pick(n, TN_RES)
    in_specs = [pl.BlockSpec((None, tm, k), lambda bi, i, j: (bi, i, 0)) for k in ks]
    assert len(set(ks)) == 1
    in_specs += [_w_spec(w, ks[0], tn, functools.partial(lambda bi, i, j, p: (layer, p, j), p=p))
                 for p in range(len(ks))]
    in_specs += [pl.BlockSpec((None, tm, tn), lambda bi, i, j: (bi, i, j)),
                 pl.BlockSpec((None, 1, tn), lambda bi, i, j: (bi, 0, j))]
    ktot = sum(ks)
    vmem = 2 * tm * ktot * 2 + 2 * ktot * tn * 2 + 5 * tm * tn * 4
    return pl.pallas_call(
        functools.partial(_matmul_res_kernel, n_a=len(a_list)),
        grid=(b, s // tm, n // tn),
        in_specs=in_specs,
        out_specs=pl.BlockSpec((None, tm, tn), lambda bi, i, j: (bi, i, j)),
        out_shape=jax.ShapeDtypeStruct((b, s, n), F32),
        compiler_params=_params(3, vmem),
        name=name,
    )(*a_list, *([w] * len(a_list)), x, gate)


def _ret_log_gamma(h):
    return math.log1p(-(2.0 ** (-5.0 - h)))


def _retention_kernel(q_ref, k_ref, v_ref, g_ref, cos_ref, sin_ref, gn_ref, o_ref, r_scr):
    ci = pl.program_id(1)
    L = q_ref.shape[0]

    @pl.when(ci == 0)
    def _():
        r_scr[...] = jnp.zeros_like(r_scr)

    cos = cos_ref[...]
    sin = sin_ref[...]
    li = lax.broadcasted_iota(jnp.int32, (L, L), 0)
    si = lax.broadcasted_iota(jnp.int32, (L, L), 1)
    rel = (li - si).astype(F32)
    causal = li >= si
    pos = lax.broadcasted_iota(jnp.int32, (L, 1), 0).astype(F32)
    half = RET_DK // 2

    def rope(t):
        t1 = t[:, :half]
        t2 = t[:, half:]
        return jnp.concatenate([t1 * cos - t2 * sin, t1 * sin + t2 * cos], axis=1)

    for h in range(RET_HEADS):
        lg = _ret_log_gamma(h)
        q = rope(q_ref[:, h * RET_DK:(h + 1) * RET_DK])
        k = rope(k_ref[:, h * RET_DK:(h + 1) * RET_DK]) * (RET_DK ** -0.5)
        v = v_ref[:, h * RET_DV:(h + 1) * RET_DV].astype(F32)
        qb = q.astype(BF16)
        decay = jnp.where(causal, jnp.exp(jnp.where(causal, rel, 0.0) * lg), 0.0)
        inner = _dot_nt(qb, k.astype(BF16)) * decay
        xi = jnp.exp(lg * (pos + 1.0))
        zeta = jnp.exp(lg * (L - 1.0 - pos))
        r_old = r_scr[h]
        y = _dot(inner.astype(BF16), v.astype(BF16)) + _dot(qb, r_old.astype(BF16)) * xi
        kt = jnp.transpose(k).astype(BF16)
        r_scr[h] = math.exp(lg * L) * r_old + _dot(kt, (v * zeta).astype(BF16))
        yc = y - jnp.mean(y, axis=-1, keepdims=True)
        yn = yc * lax.rsqrt(jnp.mean(yc * yc, axis=-1, keepdims=True) + EPS)
        cols = slice(h * RET_DV, (h + 1) * RET_DV)
        gate = _silu(g_ref[:, cols].astype(F32))
        o_ref[:, cols] = (gate * (yn * gn_ref[:, cols])).astype(o_ref.dtype)


def _retention(qk, vg, cos, sin, gn_g):
    b, s, _ = qk.shape
    L = _pick(s, L_RET)
    vmem = 2 * (2 * L * RET_QK_W * 4 + 2 * L * RET_V_W * 2 + L * RET_V_W * 2) \
        + RET_HEADS * RET_DK * RET_DV * 4 + 24 * L * RET_DV * 4
    return pl.pallas_call(
        _retention_kernel,
        grid=(b, s // L),
        in_specs=[pl.BlockSpec((None, L, RET_QK_W), lambda bi, c: (bi, c, 0)),
                  pl.BlockSpec((None, L, RET_QK_W), lambda bi, c: (bi, c, 1)),
                  pl.BlockSpec((None, L, RET_V_W), lambda bi, c: (bi, c, 0)),
                  pl.BlockSpec((None, L, RET_V_W), lambda bi, c: (bi, c, 1)),
                  pl.BlockSpec((L, RET_DK // 2), lambda bi, c: (c, 0)),
                  pl.BlockSpec((L, RET_DK // 2), lambda bi, c: (c, 0)),
                  pl.BlockSpec((1, RET_V_W), lambda bi, c: (0, 0))],
        out_specs=pl.BlockSpec((None, L, RET_V_W), lambda bi, c: (bi, c, 0)),
        out_shape=jax.ShapeDtypeStruct((b, s, RET_V_W), BF16),
        scratch_shapes=[pltpu.VMEM((RET_HEADS, RET_DK, RET_DV), F32)],
        compiler_params=_params(2, vmem),
        name="retention",
    )(qk, qk, vg, vg, cos, sin, gn_g.reshape(1, RET_V_W))


def _rms_rows(x, g):
    return x * lax.rsqrt(jnp.mean(x * x, axis=-1, keepdims=True) + EPS) * g


def _rope_slot(t, cos_t, sin_t):
    lane = lax.broadcasted_iota(jnp.int32, t.shape, 1)
    hr = MLA_ROPE // 2
    swapped = jnp.where(lane < hr, pltpu.roll(t, LANE - hr, 1), pltpu.roll(t, hr, 1))
    return t * cos_t + swapped * sin_t


def _q_up_kernel(c_ref, g_ref, w_ref, cos_ref, sin_ref, o_ref):
    h = _rms_rows(c_ref[...], g_ref[...]).astype(BF16)
    q = _dot(h, w_ref[...]) * MLA_Q_SCALE
    cos_t = cos_ref[...]
    sin_t = sin_ref[...]
    for hd in range(MLA_HEADS):
        base = hd * MLA_QK_PAD
        o_ref[:, base:base + MLA_NOPE] = q[:, base:base + MLA_NOPE].astype(o_ref.dtype)
        rs = q[:, base + MLA_NOPE:base + MLA_QK_PAD]
        o_ref[:, base + MLA_NOPE:base + MLA_QK_PAD] = _rope_slot(rs, cos_t, sin_t).astype(o_ref.dtype)


def _kv_up_kernel(c_ref, kr_ref, g_ref, wk_ref, wv_ref, cos_ref, sin_ref, k_ref, v_ref):
    h = _rms_rows(c_ref[...], g_ref[...]).astype(BF16)
    kn = _dot(h, wk_ref[...])
    v_ref[...] = _dot(h, wv_ref[...]).astype(v_ref.dtype)
    kr = _rope_slot(kr_ref[...], cos_ref[...], sin_ref[...]).astype(k_ref.dtype)
    for hd in range(MLA_HEADS):
        base = hd * MLA_QK_PAD
        k_ref[:, base:base + MLA_NOPE] = kn[:, hd * MLA_NOPE:(hd + 1) * MLA_NOPE].astype(k_ref.dtype)
        k_ref[:, base + MLA_NOPE:base + MLA_QK_PAD] = kr


def _mla_up(small, q_g, kv_g, wq, wk, wv, cos_t, sin_t):
    b, s, _ = small.shape
    tm = _pick(s, TM_UP)
    nq = MLA_HEADS * MLA_QK_PAD
    nv = MLA_HEADS * MLA_V
    tab = pl.BlockSpec((tm, LANE), lambda bi, i: (i, 0))
    q = pl.pallas_call(
        _q_up_kernel,
        grid=(b, s // tm),
        in_specs=[pl.BlockSpec((None, tm, MLA_RANK), lambda bi, i: (bi, i, 0)),
                  pl.BlockSpec((1, MLA_RANK), lambda bi, i: (0, 0)),
                  pl.BlockSpec((MLA_RANK, nq), lambda bi, i: (0, 0)),
                  tab, tab],
        out_specs=pl.BlockSpec((None, tm, nq), lambda bi, i: (bi, i, 0)),
        out_shape=jax.ShapeDtypeStruct((b, s, nq), BF16),
        compiler_params=_params(2, 2 * MLA_RANK * nq * 2 + 2 * tm * nq * 2 + 3 * tm * nq * 4),
        name="mla_q_up",
    )(small, q_g.reshape(1, MLA_RANK), wq, cos_t, sin_t)
    k, v = pl.pallas_call(
        _kv_up_kernel,
        grid=(b, s // tm),
        in_specs=[pl.BlockSpec((None, tm, MLA_RANK), lambda bi, i: (bi, i, 1)),
                  pl.BlockSpec((None, tm, LANE), lambda bi, i: (bi, i, 2 * MLA_RANK // LANE)),
                  pl.BlockSpec((1, MLA_RANK), lambda bi, i: (0, 0)),
                  pl.BlockSpec((MLA_RANK, nv), lambda bi, i: (0, 0)),
                  pl.BlockSpec((MLA_RANK, nv), lambda bi, i: (0, 0)),
                  tab, tab],
        out_specs=[pl.BlockSpec((None, tm, nq), lambda bi, i: (bi, i, 0)),
                   pl.BlockSpec((None, tm, nv), lambda bi, i: (bi, i, 0))],
        out_shape=[jax.ShapeDtypeStruct((b, s, nq), BF16),
                   jax.ShapeDtypeStruct((b, s, nv), BF16)],
        compiler_params=_params(2, 4 * MLA_RANK * nv * 2 + 2 * tm * (nq + nv) * 2 + 3 * tm * nq * 4),
        name="mla_kv_up",
    )(small, small, kv_g.reshape(1, MLA_RANK), wk, wv, cos_t, sin_t)
    return q, k, v


MLA_Q_SCALE = ((MLA_NOPE + MLA_ROPE) ** -0.5) * math.log2(math.e)


def _mla_attn_kernel(q_ref, k_ref, v_ref, o_ref, *, tk):
    qi = pl.program_id(2)
    tq = q_ref.shape[0]
    q = q_ref[...]

    def update(carry, s2, v):
        m, l, acc = carry
        m_new = jnp.maximum(m, jnp.max(s2, axis=-1, keepdims=True))
        p = jnp.exp2(s2 - m_new)
        alpha = jnp.exp2(m - m_new)
        l = alpha * l + jnp.sum(p, axis=-1, keepdims=True)
        acc = alpha * acc + _dot(p.astype(BF16), v)
        return m_new, l, acc

    def body(ki, carry):
        rows = pl.ds(pl.multiple_of(ki * tk, tk), tk)
        return update(carry, _dot_nt(q, k_ref[rows, :]), v_ref[rows, :])

    init = (jnp.full((tq, 1), NEG, F32), jnp.zeros((tq, 1), F32), jnp.zeros((tq, MLA_V), F32))
    carry = lax.fori_loop(0, qi * (tq // tk), body, init)

    qc = lax.broadcasted_iota(jnp.int32, (tq, tk), 0) // MASK_CHUNK
    kc = lax.broadcasted_iota(jnp.int32, (tq, tk), 1) // MASK_CHUNK
    for j in range(tq // tk):
        rows = pl.ds(pl.multiple_of(qi * tq + j * tk, tk), tk)
        s2 = jnp.where(kc + (j * tk) // MASK_CHUNK <= qc, _dot_nt(q, k_ref[rows, :]), NEG)
        carry = update(carry, s2, v_ref[rows, :])
    m, l, acc = carry
    o_ref[...] = (acc / l).astype(o_ref.dtype)


def _mla_attention(q, k, v):
    b, s, _ = q.shape
    tq = _pick(s, TQ_ATTN)
    tk = _pick(tq, TK_ATTN)
    vmem = 2 * (tq * MLA_QK_PAD * 2 + s * MLA_QK_PAD * 2 + s * MLA_V * 2 + tq * MLA_V * 2) \
        + 6 * tq * tk * 4
    return pl.pallas_call(
        functools.partial(_mla_attn_kernel, tk=tk),
        grid=(b, MLA_HEADS, s // tq),
        in_specs=[pl.BlockSpec((None, tq, MLA_QK_PAD), lambda bi, h, i: (bi, i, h)),
                  pl.BlockSpec((None, s, MLA_QK_PAD), lambda bi, h, i: (bi, 0, h)),
                  pl.BlockSpec((None, s, MLA_V), lambda bi, h, i: (bi, 0, h))],
        out_specs=pl.BlockSpec((None, tq, MLA_V), lambda bi, h, i: (bi, i, h)),
        out_shape=jax.ShapeDtypeStruct((b, s, MLA_HEADS * MLA_V), BF16),
        compiler_params=_params(3, vmem),
        name="mla_attention",
    )(q, k, v)


def _proj_conv_kernel(h_ref, w_ref, cw_ref, cb_ref, o_ref, halo_scr):
    si = pl.program_id(1)
    j = pl.program_id(2)
    tm, tn = o_ref.shape
    rc = _pick(tm, FFN_ROW_CHUNK)
    cc = _pick(tn, FFN_COL_CHUNK)
    for c in range(tn // cc):
        cols = slice(c * cc, (c + 1) * cc)
        prev8 = jnp.where(si == 0, 0.0, halo_scr[j, :, cols])
        for r in range(tm // rc):
            rows = slice(r * rc, (r + 1) * rc)
            raw = _dot(h_ref[rows, :], w_ref[:, cols])
            o_ref[rows, cols] = _silu(_causal_conv(raw, cw_ref.at[:, cols], cb_ref.at[:, cols], prev8))
            prev8 = raw[rc - SUBLANE:rc, :]
        halo_scr[j, :, cols] = prev8


def _proj_conv(h, w, layer, col0, conv_w, conv_b):
    b, s, k = h.shape
    n = conv_w.shape[1]
    tm = _pick(s, TM_PROJ)
    tn = _pick(n, TN_PROJ)
    nj = n // tn
    assert col0 % tn == 0
    off = col0 // tn
    vmem = 2 * tm * k * 2 + 2 * k * tn * 2 + 2 * tm * tn * 4 + 6 * tm * tn * 4
    return pl.pallas_call(
        _proj_conv_kernel,
        grid=(b, s // tm, nj),
        in_specs=[pl.BlockSpec((None, tm, k), lambda bi, i, j: (bi, i, 0)),
                  _w_spec(w, k, tn, lambda bi, i, j: (layer, 0, j + off)),
                  pl.BlockSpec((SSD_CONV, tn), lambda bi, i, j: (0, j)),
                  pl.BlockSpec((1, tn), lambda bi, i, j: (0, j))],
        out_specs=pl.BlockSpec((None, tm, tn), lambda bi, i, j: (bi, i, j)),
        out_shape=jax.ShapeDtypeStruct((b, s, n), F32),
        scratch_shapes=[pltpu.VMEM((nj, SUBLANE, tn), F32)],
        compiler_params=_params(3, vmem),
        name="ssd_in_xbc_conv",
    )(h, w, conv_w, conv_b.reshape(1, n))


def _softplus_kernel(x_ref, b_ref, o_ref):
    v = x_ref[...] + b_ref[...]
    o_ref[...] = jnp.maximum(v, 0.0) + jnp.log1p(jnp.exp(-jnp.abs(v)))


def _ssd_dt(dt_raw, dt_bias_pad):
    b, s, n = dt_raw.shape
    tm = _pick(s, 1024)
    return pl.pallas_call(
        _softplus_kernel,
        grid=(b, s // tm),
        in_specs=[pl.BlockSpec((None, tm, n), lambda bi, i: (bi, i, 0)),
                  pl.BlockSpec((1, n), lambda bi, i: (0, 0))],
        out_specs=pl.BlockSpec((None, tm, n), lambda bi, i: (bi, i, 0)),
        out_shape=jax.ShapeDtypeStruct((b, s, n), F32),
        compiler_params=_params(2, 8 * tm * n * 4),
        name="ssd_dt",
    )(dt_raw, dt_bias_pad.reshape(1, n))


def _split3(a):
    hi = a.astype(BF16)
    r1 = a - hi.astype(F32)
    mid = r1.astype(BF16)
    lo = (r1 - mid.astype(F32)).astype(BF16)
    return hi, mid, lo


SSD_PIECES = 3
SSD_QW = SSD_PIECES * SSD_HPG
SSD_NCOPY = 3 * SSD_PIECES


def _ssd_expand_matrix():
    r = np.arange(LANE)[:, None]
    c = np.arange(3 * SSD_GW)[None, :]
    hit = (r < 3 * SSD_QW) & (c // SSD_GW == r // SSD_QW) & ((c % SSD_GW) // SSD_HEADDIM == r % SSD_HPG)
    return jnp.asarray(hit, dtype=BF16)


def _ssd_scan_kernel(x_ref, b_ref, c_ref, dtc_ref, dtr_ref, ac_ref, ar_ref, dsk_ref, ex_ref,
                     o_ref, st_scr):
    ci = pl.program_id(2)
    L = x_ref.shape[0]

    @pl.when(ci == 0)
    def _():
        st_scr[...] = jnp.zeros_like(st_scr)

    li = lax.broadcasted_iota(jnp.int32, (L, L), 0)
    si = lax.broadcasted_iota(jnp.int32, (L, L), 1)
    causal = li >= si
    tril = jnp.where(causal, 1.0, 0.0).astype(BF16)
    triu = jnp.where(li <= si, 1.0, 0.0).astype(BF16)
    pad = jnp.zeros((L, LANE - SSD_NCOPY * SSD_HPG), F32)
    lane = lax.broadcasted_iota(jnp.int32, (L, LANE), 1)
    piece = (lane % SSD_QW) // SSD_HPG
    left = lane < SSD_HEADDIM

    for gi in range(dtc_ref.shape[0]):
        gcols = slice(gi * SSD_GW, (gi + 1) * SSD_GW)
        ncols = slice(gi * SSD_STATE, (gi + 1) * SSD_STATE)

        dt_c = jnp.concatenate([dtc_ref[gi], pad], axis=1)
        a_c = jnp.concatenate([-jnp.exp(ac_ref[gi]), pad[0:1]], axis=1)
        acum = sum(_dot(tril, part) for part in _split3(dt_c * a_c))
        tot = acum[L - 1:L, :]
        quantity = jnp.where(lane < SSD_QW, dt_c,
                             jnp.where(lane < 2 * SSD_QW, jnp.exp(acum), jnp.exp(tot - acum)))
        hi, mid, lo = (p.astype(F32) for p in _split3(quantity))
        cols = jnp.where(piece == 0, hi, jnp.where(piece == 1, mid, lo)).astype(BF16)
        expanded = _dot(cols, ex_ref[...])
        dt_x = expanded[:, 0:SSD_GW]
        ea_x = expanded[:, SSD_GW:2 * SSD_GW]
        te_x = expanded[:, 2 * SSD_GW:3 * SSD_GW]

        acum_r = sum(_dot(part, triu) for part in _split3(dtr_ref[gi] * -jnp.exp(ar_ref[gi])))

        xs = x_ref[:, gcols]
        bm = b_ref[:, ncols]
        cm = c_ref[:, ncols].astype(BF16)
        cb = _dot_nt(cm, bm.astype(BF16))
        state = st_scr[gi]
        cs = _dot(cm, state.astype(BF16))
        bmt = jnp.transpose(bm).astype(BF16)
        xdt = xs * dt_x

        ys = []
        for t in range(SSD_GW // LANE):
            x2 = xdt[:, t * LANE:(t + 1) * LANE]
            y_t = None
            for h, keep in ((2 * t, left), (2 * t + 1, jnp.logical_not(left))):
                seg = acum[:, h:h + 1] - acum_r[h:h + 1, :]
                lmat = jnp.exp(jnp.where(causal, seg, NEG))
                part = _dot((cb * lmat).astype(BF16), jnp.where(keep, x2, 0.0).astype(BF16))
                y_t = part if y_t is None else y_t + part
            ys.append(y_t)
        y_diag = jnp.concatenate(ys, axis=1)
        o_ref[:, gcols] = y_diag + cs * ea_x + xs * dsk_ref[:, gcols]
        st_scr[gi] = state * ea_x[L - 1:L, :] + _dot(bmt, (xdt * te_x).astype(BF16))


def _ssd_scan(xbc, dtp, a_log, d_skip):
    b, s, _ = xbc.shape
    L = _pick(s, L_SSD)
    g = SSD_GROUPS
    gps = SSD_GROUPS_PER_STEP
    dt_g = dtp.reshape(b, s, g, SSD_HPG)
    dtc = jnp.tile(jnp.transpose(dt_g, (0, 2, 1, 3)), (1, 1, 1, SSD_NCOPY))
    dtr = jnp.transpose(dt_g, (0, 2, 3, 1))
    a_col = jnp.tile(a_log.reshape(g, 1, SSD_HPG), (1, 1, SSD_NCOPY))
    a_row = a_log.reshape(g, SSD_HPG, 1)
    ncol = SSD_NCOPY * SSD_HPG
    dsk = jnp.repeat(d_skip, SSD_HEADDIM).reshape(1, SSD_INNER)
    xw = gps * SSD_GW
    nw = gps * SSD_STATE
    boff = SSD_INNER // nw
    coff = boff + g // gps
    vmem = gps * (4 * L * SSD_GW * 4 + 24 * L * L * 4 + 16 * L * SSD_GW * 4)
    return pl.pallas_call(
        _ssd_scan_kernel,
        grid=(b, g // gps, s // L),
        in_specs=[pl.BlockSpec((None, L, xw), lambda bi, gi, c: (bi, c, gi)),
                  pl.BlockSpec((None, L, nw), lambda bi, gi, c: (bi, c, boff + gi)),
                  pl.BlockSpec((None, L, nw), lambda bi, gi, c: (bi, c, coff + gi)),
                  pl.BlockSpec((None, gps, L, ncol), lambda bi, gi, c: (bi, gi, c, 0)),
                  pl.BlockSpec((None, gps, SSD_HPG, L), lambda bi, gi, c: (bi, gi, 0, c)),
                  pl.BlockSpec((gps, 1, ncol), lambda bi, gi, c: (gi, 0, 0)),
                  pl.BlockSpec((gps, SSD_HPG, 1), lambda bi, gi, c: (gi, 0, 0)),
                  pl.BlockSpec((1, xw), lambda bi, gi, c: (0, gi)),
                  pl.BlockSpec((LANE, 3 * SSD_GW), lambda bi, gi, c: (0, 0))],
        out_specs=pl.BlockSpec((None, L, xw), lambda bi, gi, c: (bi, c, gi)),
        out_shape=jax.ShapeDtypeStruct((b, s, SSD_INNER), F32),
        scratch_shapes=[pltpu.VMEM((gps, SSD_STATE, SSD_GW), F32)],
        compiler_params=_params(3, vmem),
        name="ssd_scan",
    )(xbc, xbc, xbc, dtc, dtr, a_col, a_row, dsk, _ssd_expand_matrix())


def _gate_norm_kernel(y_ref, z_ref, g_ref, o_ref):
    v = y_ref[...] * _silu(z_ref[...].astype(F32))
    o_ref[...] = _rms_rows(v, g_ref[...]).astype(o_ref.dtype)


def _gate_norm(y, z, g):
    b, s, n = y.shape
    tm = _pick(s, 256)
    return pl.pallas_call(
        _gate_norm_kernel,
        grid=(b, s // tm),
        in_specs=[pl.BlockSpec((None, tm, n), lambda bi, i: (bi, i, 0)),
                  pl.BlockSpec((None, tm, n), lambda bi, i: (bi, i, 0)),
                  pl.BlockSpec((1, n), lambda bi, i: (0, 0))],
        out_specs=pl.BlockSpec((None, tm, n), lambda bi, i: (bi, i, 0)),
        out_shape=jax.ShapeDtypeStruct((b, s, n), BF16),
        compiler_params=_params(2, 8 * tm * n * 4),
        name="ssd_gate_norm",
    )(y, z, g.reshape(1, n))


def _final_norm_kernel(x_ref, g_ref, o_ref):
    o_ref[...] = _rms_rows(x_ref[...], g_ref[...])


def _final_norm(x, g):
    b, s, n = x.shape
    tm = _pick(s, 512)
    return pl.pallas_call(
        _final_norm_kernel,
        grid=(b, s // tm),
        in_specs=[pl.BlockSpec((None, tm, n), lambda bi, i: (bi, i, 0)),
                  pl.BlockSpec((1, n), lambda bi, i: (0, 0))],
        out_specs=pl.BlockSpec((None, tm, n), lambda bi, i: (bi, i, 0)),
        out_shape=jax.ShapeDtypeStruct((b, s, n), F32),
        compiler_params=_params(2, 8 * tm * n * 4),
        name="final_norm",
    )(x, g.reshape(1, n))


def _rope_tables(s):
    pos = jnp.arange(s, dtype=jnp.int32).astype(F32)[:, None]
    half = RET_DK // 2
    inv = RET_THETA ** (-jnp.arange(half, dtype=F32) / half)
    ang = pos * inv[None, :]
    ret_cos, ret_sin = jnp.cos(ang), jnp.sin(ang)
    hr = MLA_ROPE // 2
    inv_m = MLA_THETA ** (-jnp.arange(hr, dtype=F32) / hr)
    ang_m = pos * inv_m[None, :]
    cm, sm = jnp.cos(ang_m), jnp.sin(ang_m)
    pad = LANE - MLA_ROPE
    mla_cos = jnp.concatenate([cm, cm, jnp.ones((s, pad), F32)], axis=1)
    mla_sin = jnp.concatenate([-sm, sm, jnp.zeros((s, pad), F32)], axis=1)
    return ret_cos, ret_sin, mla_cos, mla_sin


def _hybrid_layer(x, mods, norm_g, w_in, layer, q_g, w_uq, kv_g, w_ukv, gn_g, w_out, tables):
    sh, sc, gt = mods
    ret_cos, ret_sin, mla_cos, mla_sin = tables
    o_v = 2 * RET_QK_W
    o_c = o_v + 2 * RET_V_W
    o_kr = o_c + 2 * MLA_RANK
    w_small = jnp.concatenate(
        [w_in[layer, :, o_c:o_kr + MLA_ROPE], jnp.zeros((D_MODEL, LANE - MLA_ROPE), BF16)], axis=1)
    h = _norm_mod(x, norm_g, sh, sc)
    qk = _matmul(h, w_in, layer, 0, o_v, F32, "hyb_in_qk")
    vg = _matmul(h, w_in, layer, o_v, o_c - o_v, BF16, "hyb_in_vg")
    small = _matmul(h, w_small, None, 0, w_small.shape[1], F32, "hyb_in_latent")
    y_ret = _retention(qk, vg, ret_cos, ret_sin, gn_g)

    wq = w_uq.astype(BF16).reshape(MLA_RANK, MLA_HEADS, MLA_NOPE + MLA_ROPE)
    wq = jnp.pad(wq, ((0, 0), (0, 0), (0, MLA_QK_PAD - MLA_NOPE - MLA_ROPE)))
    wq = wq.reshape(MLA_RANK, MLA_HEADS * MLA_QK_PAD)
    wkv = w_ukv.astype(BF16).reshape(MLA_RANK, MLA_HEADS, MLA_NOPE + MLA_V)
    wk = wkv[:, :, :MLA_NOPE].reshape(MLA_RANK, MLA_HEADS * MLA_NOPE)
    wv = wkv[:, :, MLA_NOPE:].reshape(MLA_RANK, MLA_HEADS * MLA_V)
    q, k, v = _mla_up(small, q_g, kv_g, wq, wk, wv, mla_cos, mla_sin)
    y_mla = _mla_attention(q, k, v)
    return _matmul_res([y_ret, y_mla], w_out, layer, x, gt, "hyb_out")


def _ssd_layer(x, mods, norm_g, w_in, layer, conv_w, conv_b, dt_bias, a_log, d_skip, ssd_g, w_out):
    sh, sc, gt = mods
    o_x = SSD_INNER
    o_dt = o_x + SSD_CONV_DIM
    w_dt = jnp.concatenate(
        [w_in[layer, :, o_dt:], jnp.zeros((D_MODEL, LANE - SSD_HEADS), BF16)], axis=1)
    h = _norm_mod(x, norm_g, sh, sc)
    z = _matmul(h, w_in, layer, 0, o_x, BF16, "ssd_in_z")
    xbc = _proj_conv(h, w_in, layer, o_x, conv_w, conv_b)
    dt_raw = _matmul(h, w_dt, None, 0, LANE, F32, "ssd_in_dt")
    dtp = _ssd_dt(dt_raw, jnp.pad(dt_bias, (0, LANE - SSD_HEADS)))[:, :, :SSD_HEADS]
    y = _ssd_scan(xbc, dtp, a_log, d_skip)
    hn = _gate_norm(y, z, ssd_g)
    return _matmul_res([hn], w_out, layer, x, gt, "ssd_out")


def _ffn_layer(x, mods, norm_g, w_up, w_down, layer, conv_w, conv_b):
    sh, sc, gt = mods
    a = _ffn_up(_norm_mod(x, norm_g, sh, sc), w_up, layer, conv_w, conv_b)
    return _matmul_res([a], w_down, layer, x, gt, "ffn_down")


def kernel(x, c, ada_w, ada_b, norm_mix_g, norm_ffn_g, hyb_w_in, hyb_q_norm_g, hyb_w_uq, hyb_kv_norm_g, hyb_w_ukv, hyb_ret_gn_g, hyb_w_out, ssd_w_in, ssd_conv_w, ssd_conv_b, ssd_dt_bias, ssd_a_log, ssd_d, ssd_norm_g, ssd_w_out, ffn_w_up, ffn_conv_w, ffn_conv_b, ffn_w_down, final_norm_g):
    b, s, d = x.shape
    depth = ada_w.shape[0]
    mods = _mods(c, ada_w, ada_b).reshape(depth, b, 6, 1, d)
    tables = _rope_tables(s)
    hyb_w_in, hyb_w_out, ssd_w_in, ssd_w_out, ffn_w_up, ffn_w_down = (
        w.astype(BF16) for w in (hyb_w_in, hyb_w_out, ssd_w_in, ssd_w_out, ffn_w_up, ffn_w_down))
    for l in range(depth):
        m = [mods[l, :, k] for k in range(6)]
        i = l // 2
        if l % 2 == 0:
            x = _hybrid_layer(x, m[0:3], norm_mix_g[l], hyb_w_in, i, hyb_q_norm_g[i], hyb_w_uq[i],
                              hyb_kv_norm_g[i], hyb_w_ukv[i], hyb_ret_gn_g[i], hyb_w_out, tables)
        else:
            x = _ssd_layer(x, m[0:3], norm_mix_g[l], ssd_w_in, i, ssd_conv_w[i], ssd_conv_b[i],
                           ssd_dt_bias[i], ssd_a_log[i], ssd_d[i], ssd_norm_g[i], ssd_w_out)
        x = _ffn_layer(x, m[3:6], norm_ffn_g[l], ffn_w_up, ffn_w_down, l, ffn_conv_w[l], ffn_conv_b[l])
    return _final_norm(x, final_norm_g)
```

```python
import functools
import math

import numpy as np
import jax
import jax.numpy as jnp
from jax import lax
from jax.experimental import pallas as pl
from jax.experimental.pallas import tpu as pltpu

F32 = jnp.float32
BF16 = jnp.bfloat16

D_MODEL = 2048
DEPTH = 4
EPS = 1e-6

RET_HEADS = 4
RET_DK = 256
RET_DV = 512
RET_THETA = 10000.0
RET_QK_W = RET_HEADS * RET_DK
RET_V_W = RET_HEADS * RET_DV

MLA_HEADS = 16
MLA_NOPE = 128
MLA_ROPE = 64
MLA_V = 128
MLA_RANK = 512
MLA_THETA = 10000.0
MLA_QK_PAD = 256
MASK_CHUNK = 64

SSD_INNER = 4096
SSD_HEADDIM = 64
SSD_HEADS = 64
SSD_GROUPS = 8
SSD_HPG = SSD_HEADS // SSD_GROUPS
SSD_STATE = 128
SSD_CONV = 4
SSD_CONV_DIM = SSD_INNER + 2 * SSD_GROUPS * SSD_STATE
SSD_GW = SSD_HPG * SSD_HEADDIM

FFN_HIDDEN = 5632
FFN_CONV = 3

LANE = 128
SUBLANE = 8
VMEM_CAP = 56 * 1024 * 1024
NEG = -1e30

TM_PROJ = 1024
TM_MATMUL = 2048
TN_PROJ = 1024
TH_FFN = 512
FFN_ROW_CHUNK = 256
FFN_COL_CHUNK = 256
TM_NORM = 2048
NORM_ROW_CHUNK = 16
TN_RES = 512
L_RET = 256
L_SSD = 256
SSD_GROUPS_PER_STEP = 2
TQ_ATTN = 1024
TK_ATTN = 1024
TM_UP = 512


def _pick(n, pref):
    t = pref
    while t > 1 and n % t:
        t //= 2
    return t if n % t == 0 else n


def _params(n_axes, vmem_bytes):
    limit = int(min(VMEM_CAP, max(16 * 1024 * 1024, vmem_bytes * 1.3 + (4 << 20))))
    return pltpu.CompilerParams(dimension_semantics=("arbitrary",) * n_axes,
                                vmem_limit_bytes=limit)


def _silu(v):
    return v * (1.0 / (1.0 + jnp.exp(-v)))


def _dot(a, b):
    return jnp.dot(a, b, preferred_element_type=F32)


def _dot_nt(a, b):
    return lax.dot_general(a, b, (((1,), (1,)), ((), ())), preferred_element_type=F32)


def _shift_rows(g, k, prev8):
    c = g.shape[1]
    r = pltpu.roll(g, k, 0)
    hr = pltpu.roll(prev8, k, 0)
    rows = lax.broadcasted_iota(jnp.int32, (SUBLANE, c), 0)
    top = jnp.where(rows < k, hr, r[0:SUBLANE])
    if g.shape[0] == SUBLANE:
        return top
    return jnp.concatenate([top, r[SUBLANE:]], axis=0)


def _causal_conv(g, w_ref, b_ref, prev8):
    width = w_ref.shape[0]
    y = g * w_ref[width - 1:width, :] + b_ref[...]
    for k in range(1, width):
        y = y + _shift_rows(g, k, prev8) * w_ref[width - 1 - k:width - k, :]
    return y


def _mods_kernel(c_ref, w_ref, b_ref, o_ref):
    h = _silu(c_ref[...]).astype(BF16)
    o_ref[...] = _dot(h, w_ref[...].astype(BF16)) + b_ref[...]


def _mods(c, ada_w, ada_b):
    depth, d, n = ada_w.shape
    b = c.shape[0]
    tn = _pick(n, 1024)
    return pl.pallas_call(
        _mods_kernel,
        grid=(depth, n // tn),
        in_specs=[pl.BlockSpec((b, d), lambda l, j: (0, 0)),
                  pl.BlockSpec((None, d, tn), lambda l, j: (l, 0, j)),
                  pl.BlockSpec((None, 1, tn), lambda l, j: (l, 0, j))],
        out_specs=pl.BlockSpec((None, b, tn), lambda l, j: (l, 0, j)),
        out_shape=jax.ShapeDtypeStruct((depth, b, n), F32),
        compiler_params=_params(2, 2 * d * tn * 4 + d * tn * 2),
        name="adaln_mods",
    )(c, ada_w, ada_b.reshape(depth, 1, n))


def _norm_mod_kernel(x_ref, g_ref, sh_ref, sc_ref, o_ref):
    rc = NORM_ROW_CHUNK
    gs = g_ref[...] * (1.0 + sc_ref[...])
    sh = sh_ref[...]

    def body(r, carry):
        rows = pl.ds(pl.multiple_of(r * rc, rc), rc)
        xf = x_ref[rows, :]
        inv = lax.rsqrt(jnp.mean(xf * xf, axis=-1, keepdims=True) + EPS)
        o_ref[rows, :] = ((xf * inv) * gs + sh).astype(o_ref.dtype)
        return carry

    lax.fori_loop(0, x_ref.shape[0] // rc, body, 0, unroll=4)


def _norm_mod(x, g, shift, scale):
    b, s, d = x.shape
    tm = _pick(s, TM_NORM)
    return pl.pallas_call(
        _norm_mod_kernel,
        grid=(b, s // tm),
        in_specs=[pl.BlockSpec((None, tm, d), lambda bi, i: (bi, i, 0)),
                  pl.BlockSpec((1, d), lambda bi, i: (0, 0)),
                  pl.BlockSpec((None, 1, d), lambda bi, i: (bi, 0, 0)),
                  pl.BlockSpec((None, 1, d), lambda bi, i: (bi, 0, 0))],
        out_specs=pl.BlockSpec((None, tm, d), lambda bi, i: (bi, i, 0)),
        out_shape=jax.ShapeDtypeStruct((b, s, d), BF16),
        compiler_params=_params(2, 8 * tm * d * 4),
        name="norm_mod",
    )(x, g.reshape(1, d), shift, scale)


def _matmul_kernel(h_ref, w_ref, o_ref):
    o_ref[...] = _dot(h_ref[...], w_ref[...].astype(BF16)).astype(o_ref.dtype)


def _w_spec(w, rows, cols, index):
    if w.ndim == 2:
        return pl.BlockSpec((rows, cols), lambda bi, i, j: index(bi, i, j)[1:])
    return pl.BlockSpec((None, rows, cols), index)


def _matmul(h, w, layer, col0, n, out_dtype, name):
    b, s, k = h.shape
    tm = _pick(s, TM_PROJ)
    tn = _pick(n, TN_PROJ) if n % TN_PROJ == 0 else n
    assert col0 % tn == 0
    off = col0 // tn
    osz = jnp.dtype(out_dtype).itemsize
    vmem = 2 * tm * k * 2 + 2 * k * tn * w.dtype.itemsize + k * tn * 2 + 2 * tm * tn * osz + tm * tn * 4
    return pl.pallas_call(
        _matmul_kernel,
        grid=(b, s // tm, n // tn),
        in_specs=[pl.BlockSpec((None, tm, k), lambda bi, i, j: (bi, i, 0)),
                  _w_spec(w, k, tn, lambda bi, i, j: (layer, 0, j + off))],
        out_specs=pl.BlockSpec((None, tm, tn), lambda bi, i, j: (bi, i, j)),
        out_shape=jax.ShapeDtypeStruct((b, s, n), out_dtype),
        compiler_params=_params(3, vmem),
        name=name,
    )(h, w)


def _ffn_up_kernel(h_ref, wu_ref, wg_ref, cw_ref, cb_ref, o_ref, halo_scr):
    si = pl.program_id(1)
    j = pl.program_id(2)
    tm, th = o_ref.shape
    rc = _pick(tm, FFN_ROW_CHUNK)
    cc = _pick(th, FFN_COL_CHUNK)
    for c in range(th // cc):
        cols = slice(c * cc, (c + 1) * cc)
        prev8 = jnp.where(si == 0, 0.0, halo_scr[j, :, cols])
        wu = wu_ref[:, cols].astype(BF16)
        wg = wg_ref[:, cols].astype(BF16)
        for r in range(tm // rc):
            rows = slice(r * rc, (r + 1) * rc)
            hr = h_ref[rows, :]
            u = _dot(hr, wu)
            gate = _dot(hr, wg)
            gc = _causal_conv(gate, cw_ref.at[:, cols], cb_ref.at[:, cols], prev8)
            o_ref[rows, cols] = (_silu(gc) * u).astype(o_ref.dtype)
            prev8 = gate[rc - SUBLANE:rc, :]
        halo_scr[j, :, cols] = prev8


def _ffn_up(h, w_up, layer, conv_w, conv_b):
    b, s, d = h.shape
    hid = conv_w.shape[1]
    tm = _pick(s, TM_MATMUL)
    th = _pick(hid, TH_FFN)
    nj = hid // th
    vmem = 2 * tm * d * 2 + 4 * d * th * w_up.dtype.itemsize + 2 * d * th * 2 + 2 * tm * th * 2 + 4 * tm * th * 4
    return pl.pallas_call(
        _ffn_up_kernel,
        grid=(b, s // tm, nj),
        in_specs=[pl.BlockSpec((None, tm, d), lambda bi, i, j: (bi, i, 0)),
                  _w_spec(w_up, d, th, lambda bi, i, j: (layer, 0, j)),
                  _w_spec(w_up, d, th, lambda bi, i, j: (layer, 0, j + nj)),
                  pl.BlockSpec((FFN_CONV, th), lambda bi, i, j: (0, j)),
                  pl.BlockSpec((1, th), lambda bi, i, j: (0, j))],
        out_specs=pl.BlockSpec((None, tm, th), lambda bi, i, j: (bi, i, j)),
        out_shape=jax.ShapeDtypeStruct((b, s, hid), BF16),
        scratch_shapes=[pltpu.VMEM((nj, SUBLANE, th), F32)],
        compiler_params=_params(3, vmem),
        name="ffn_up",
    )(h, w_up, w_up, conv_w, conv_b.reshape(1, hid))


def _matmul_res_kernel(*refs, n_a):
    a_refs = refs[:n_a]
    w_refs = refs[n_a:2 * n_a]
    x_ref, gt_ref, o_ref = refs[2 * n_a:]
    acc = _dot(a_refs[0][...], w_refs[0][...].astype(BF16))
    for a_ref, w_ref in zip(a_refs[1:], w_refs[1:]):
        acc = acc + _dot(a_ref[...], w_ref[...].astype(BF16))
    o_ref[...] = x_ref[...] + gt_ref[...] * acc


def _matmul_res(a_list, w, layer, x, gate, name):
    b, s, n = x.shape
    ks = [a.shape[2] for a in a_list]
    tm = _pick(s, TM_PROJ)
    tn = _pick(n, TN_RES)
    in_specs = [pl.BlockSpec((None, tm, k), lambda bi, i, j: (bi, i, 0)) for k in ks]
    assert len(set(ks)) == 1
    in_specs += [_w_spec(w, ks[0], tn, functools.partial(lambda bi, i, j, p: (layer, p, j), p=p))
                 for p in range(len(ks))]
    in_specs += [pl.BlockSpec((None, tm, tn), lambda bi, i, j: (bi, i, j)),
                 pl.BlockSpec((None, 1, tn), lambda bi, i, j: (bi, 0, j))]
    ktot = sum(ks)
    vmem = 2 * tm * ktot * 2 + 2 * ktot * tn * w.dtype.itemsize + ktot * tn * 2 + 5 * tm * tn * 4
    return pl.pallas_call(
        functools.partial(_matmul_res_kernel, n_a=len(a_list)),
        grid=(b, s // tm, n // tn),
        in_specs=in_specs,
        out_specs=pl.BlockSpec((None, tm, tn), lambda bi, i, j: (bi, i, j)),
        out_shape=jax.ShapeDtypeStruct((b, s, n), F32),
        compiler_params=_params(3, vmem),
        name=name,
    )(*a_list, *([w] * len(a_list)), x, gate)


def _ret_log_gamma(h):
    return math.log1p(-(2.0 ** (-5.0 - h)))


def _retention_kernel(q_ref, k_ref, v_ref, g_ref, cos_ref, sin_ref, gn_ref, o_ref, r_scr):
    ci = pl.program_id(1)
    L = q_ref.shape[0]

    @pl.when(ci == 0)
    def _():
        r_scr[...] = jnp.zeros_like(r_scr)

    cos = cos_ref[...]
    sin = sin_ref[...]
    li = lax.broadcasted_iota(jnp.int32, (L, L), 0)
    si = lax.broadcasted_iota(jnp.int32, (L, L), 1)
    rel = (li - si).astype(F32)
    causal = li >= si
    pos = lax.broadcasted_iota(jnp.int32, (L, 1), 0).astype(F32)
    half = RET_DK // 2

    def rope(t):
        t1 = t[:, :half]
        t2 = t[:, half:]
        return jnp.concatenate([t1 * cos - t2 * sin, t1 * sin + t2 * cos], axis=1)

    for h in range(RET_HEADS):
        lg = _ret_log_gamma(h)
        q = rope(q_ref[:, h * RET_DK:(h + 1) * RET_DK])
        k = rope(k_ref[:, h * RET_DK:(h + 1) * RET_DK]) * (RET_DK ** -0.5)
        v = v_ref[:, h * RET_DV:(h + 1) * RET_DV].astype(F32)
        qb = q.astype(BF16)
        decay = jnp.where(causal, jnp.exp(jnp.where(causal, rel, 0.0) * lg), 0.0)
        inner = _dot_nt(qb, k.astype(BF16)) * decay
        xi = jnp.exp(lg * (pos + 1.0))
        zeta = jnp.exp(lg * (L - 1.0 - pos))
        r_old = r_scr[h]
        y = _dot(inner.astype(BF16), v.astype(BF16)) + _dot(qb, r_old.astype(BF16)) * xi
        kt = jnp.transpose(k).astype(BF16)
        r_scr[h] = math.exp(lg * L) * r_old + _dot(kt, (v * zeta).astype(BF16))
        yc = y - jnp.mean(y, axis=-1, keepdims=True)
        yn = yc * lax.rsqrt(jnp.mean(yc * yc, axis=-1, keepdims=True) + EPS)
        cols = slice(h * RET_DV, (h + 1) * RET_DV)
        gate = _silu(g_ref[:, cols].astype(F32))
        o_ref[:, cols] = (gate * (yn * gn_ref[:, cols])).astype(o_ref.dtype)


def _retention(qk, vg, cos, sin, gn_g):
    b, s, _ = qk.shape
    L = _pick(s, L_RET)
    vmem = 2 * (2 * L * RET_QK_W * 4 + 2 * L * RET_V_W * 2 + L * RET_V_W * 2) \
        + RET_HEADS * RET_DK * RET_DV * 4 + 24 * L * RET_DV * 4
    return pl.pallas_call(
        _retention_kernel,
        grid=(b, s // L),
        in_specs=[pl.BlockSpec((None, L, RET_QK_W), lambda bi, c: (bi, c, 0)),
                  pl.BlockSpec((None, L, RET_QK_W), lambda bi, c: (bi, c, 1)),
                  pl.BlockSpec((None, L, RET_V_W), lambda bi, c: (bi, c, 0)),
                  pl.BlockSpec((None, L, RET_V_W), lambda bi, c: (bi, c, 1)),
                  pl.BlockSpec((L, RET_DK // 2), lambda bi, c: (c, 0)),
                  pl.BlockSpec((L, RET_DK // 2), lambda bi, c: (c, 0)),
                  pl.BlockSpec((1, RET_V_W), lambda bi, c: (0, 0))],
        out_specs=pl.BlockSpec((None, L, RET_V_W), lambda bi, c: (bi, c, 0)),
        out_shape=jax.ShapeDtypeStruct((b, s, RET_V_W), BF16),
        scratch_shapes=[pltpu.VMEM((RET_HEADS, RET_DK, RET_DV), F32)],
        compiler_params=_params(2, vmem),
        name="retention",
    )(qk, qk, vg, vg, cos, sin, gn_g.reshape(1, RET_V_W))


def _rms_rows(x, g):
    return x * lax.rsqrt(jnp.mean(x * x, axis=-1, keepdims=True) + EPS) * g


def _rope_slot(t, cos_t, sin_t):
    lane = lax.broadcasted_iota(jnp.int32, t.shape, 1)
    hr = MLA_ROPE // 2
    swapped = jnp.where(lane < hr, pltpu.roll(t, LANE - hr, 1), pltpu.roll(t, hr, 1))
    return t * cos_t + swapped * sin_t


def _q_up_kernel(c_ref, g_ref, w_ref, cos_ref, sin_ref, o_ref):
    h = _rms_rows(c_ref[...], g_ref[...]).astype(BF16)
    q = _dot(h, w_ref[...]) * MLA_Q_SCALE
    cos_t = cos_ref[...]
    sin_t = sin_ref[...]
    for hd in range(MLA_HEADS):
        base = hd * MLA_QK_PAD
        o_ref[:, base:base + MLA_NOPE] = q[:, base:base + MLA_NOPE].astype(o_ref.dtype)
        rs = q[:, base + MLA_NOPE:base + MLA_QK_PAD]
        o_ref[:, base + MLA_NOPE:base + MLA_QK_PAD] = _rope_slot(rs, cos_t, sin_t).astype(o_ref.dtype)


def _kv_up_kernel(c_ref, kr_ref, g_ref, wk_ref, wv_ref, cos_ref, sin_ref, k_ref, v_ref):
    h = _rms_rows(c_ref[...], g_ref[...]).astype(BF16)
    kn = _dot(h, wk_ref[...])
    v_ref[...] = _dot(h, wv_ref[...]).astype(v_ref.dtype)
    kr = _rope_slot(kr_ref[...], cos_ref[...], sin_ref[...]).astype(k_ref.dtype)
    for hd in range(MLA_HEADS):
        base = hd * MLA_QK_PAD
        k_ref[:, base:base + MLA_NOPE] = kn[:, hd * MLA_NOPE:(hd + 1) * MLA_NOPE].astype(k_ref.dtype)
        k_ref[:, base + MLA_NOPE:base + MLA_QK_PAD] = kr


def _mla_up(small, q_g, kv_g, wq, wk, wv, cos_t, sin_t):
    b, s, _ = small.shape
    tm = _pick(s, TM_UP)
    nq = MLA_HEADS * MLA_QK_PAD
    nv = MLA_HEADS * MLA_V
    tab = pl.BlockSpec((tm, LANE), lambda bi, i: (i, 0))
    q = pl.pallas_call(
        _q_up_kernel,
        grid=(b, s // tm),
        in_specs=[pl.BlockSpec((None, tm, MLA_RANK), lambda bi, i: (bi, i, 0)),
                  pl.BlockSpec((1, MLA_RANK), lambda bi, i: (0, 0)),
                  pl.BlockSpec((MLA_RANK, nq), lambda bi, i: (0, 0)),
                  tab, tab],
        out_specs=pl.BlockSpec((None, tm, nq), lambda bi, i: (bi, i, 0)),
        out_shape=jax.ShapeDtypeStruct((b, s, nq), BF16),
        compiler_params=_params(2, 2 * MLA_RANK * nq * 2 + 2 * tm * nq * 2 + 3 * tm * nq * 4),
        name="mla_q_up",
    )(small, q_g.reshape(1, MLA_RANK), wq, cos_t, sin_t)
    k, v = pl.pallas_call(
        _kv_up_kernel,
        grid=(b, s // tm),
        in_specs=[pl.BlockSpec((None, tm, MLA_RANK), lambda bi, i: (bi, i, 1)),
                  pl.BlockSpec((None, tm, LANE), lambda bi, i: (bi, i, 2 * MLA_RANK // LANE)),
                  pl.BlockSpec((1, MLA_RANK), lambda bi, i: (0, 0)),
                  pl.BlockSpec((MLA_RANK, nv), lambda bi, i: (0, 0)),
                  pl.BlockSpec((MLA_RANK, nv), lambda bi, i: (0, 0)),
                  tab, tab],
        out_specs=[pl.BlockSpec((None, tm, nq), lambda bi, i: (bi, i, 0)),
                   pl.BlockSpec((None, tm, nv), lambda bi, i: (bi, i, 0))],
        out_shape=[jax.ShapeDtypeStruct((b, s, nq), BF16),
                   jax.ShapeDtypeStruct((b, s, nv), BF16)],
        compiler_params=_params(2, 4 * MLA_RANK * nv * 2 + 2 * tm * (nq + nv) * 2 + 3 * tm * nq * 4),
        name="mla_kv_up",
    )(small, small, kv_g.reshape(1, MLA_RANK), wk, wv, cos_t, sin_t)
    return q, k, v


MLA_Q_SCALE = ((MLA_NOPE + MLA_ROPE) ** -0.5) * math.log2(math.e)


def _mla_attn_kernel(q_ref, k_ref, v_ref, o_ref, *, tk):
    qi = pl.program_id(2)
    tq = q_ref.shape[0]
    q = q_ref[...]

    def update(carry, s2, v):
        m, l, acc = carry
        m_new = jnp.maximum(m, jnp.max(s2, axis=-1, keepdims=True))
        p = jnp.exp2(s2 - m_new)
        alpha = jnp.exp2(m - m_new)
        l = alpha * l + jnp.sum(p, axis=-1, keepdims=True)
        acc = alpha * acc + _dot(p.astype(BF16), v)
        return m_new, l, acc

    def body(ki, carry):
        rows = pl.ds(pl.multiple_of(ki * tk, tk), tk)
        return update(carry, _dot_nt(q, k_ref[rows, :]), v_ref[rows, :])

    init = (jnp.full((tq, 1), NEG, F32), jnp.zeros((tq, 1), F32), jnp.zeros((tq, MLA_V), F32))
    carry = lax.fori_loop(0, qi * (tq // tk), body, init)

    qc = lax.broadcasted_iota(jnp.int32, (tq, tk), 0) // MASK_CHUNK
    kc = lax.broadcasted_iota(jnp.int32, (tq, tk), 1) // MASK_CHUNK
    for j in range(tq // tk):
        rows = pl.ds(pl.multiple_of(qi * tq + j * tk, tk), tk)
        s2 = jnp.where(kc + (j * tk) // MASK_CHUNK <= qc, _dot_nt(q, k_ref[rows, :]), NEG)
        carry = update(carry, s2, v_ref[rows, :])
    m, l, acc = carry
    o_ref[...] = (acc / l).astype(o_ref.dtype)


def _mla_attention(q, k, v):
    b, s, _ = q.shape
    tq = _pick(s, TQ_ATTN)
    tk = _pick(tq, TK_ATTN)
    vmem = 2 * (tq * MLA_QK_PAD * 2 + s * MLA_QK_PAD * 2 + s * MLA_V * 2 + tq * MLA_V * 2) \
        + 6 * tq * tk * 4
    return pl.pallas_call(
        functools.partial(_mla_attn_kernel, tk=tk),
        grid=(b, MLA_HEADS, s // tq),
        in_specs=[pl.BlockSpec((None, tq, MLA_QK_PAD), lambda bi, h, i: (bi, i, h)),
                  pl.BlockSpec((None, s, MLA_QK_PAD), lambda bi, h, i: (bi, 0, h)),
                  pl.BlockSpec((None, s, MLA_V), lambda bi, h, i: (bi, 0, h))],
        out_specs=pl.BlockSpec((None, tq, MLA_V), lambda bi, h, i: (bi, i, h)),
        out_shape=jax.ShapeDtypeStruct((b, s, MLA_HEADS * MLA_V), BF16),
        compiler_params=_params(3, vmem),
        name="mla_attention",
    )(q, k, v)


def _proj_conv_kernel(h_ref, w_ref, cw_ref, cb_ref, o_ref, halo_scr):
    si = pl.program_id(1)
    j = pl.program_id(2)
    tm, tn = o_ref.shape
    rc = _pick(tm, FFN_ROW_CHUNK)
    cc = _pick(tn, FFN_COL_CHUNK)
    for c in range(tn // cc):
        cols = slice(c * cc, (c + 1) * cc)
        prev8 = jnp.where(si == 0, 0.0, halo_scr[j, :, cols])
        wc = w_ref[:, cols].astype(BF16)
        for r in range(tm // rc):
            rows = slice(r * rc, (r + 1) * rc)
            raw = _dot(h_ref[rows, :], wc)
            o_ref[rows, cols] = _silu(_causal_conv(raw, cw_ref.at[:, cols], cb_ref.at[:, cols], prev8))
            prev8 = raw[rc - SUBLANE:rc, :]
        halo_scr[j, :, cols] = prev8


def _proj_conv(h, w, layer, col0, conv_w, conv_b):
    b, s, k = h.shape
    n = conv_w.shape[1]
    tm = _pick(s, TM_PROJ)
    tn = _pick(n, TN_PROJ)
    nj = n // tn
    assert col0 % tn == 0
    off = col0 // tn
    vmem = 2 * tm * k * 2 + 2 * k * tn * w.dtype.itemsize + k * tn * 2 + 2 * tm * tn * 4 + 4 * tm * tn * 4
    return pl.pallas_call(
        _proj_conv_kernel,
        grid=(b, s // tm, nj),
        in_specs=[pl.BlockSpec((None, tm, k), lambda bi, i, j: (bi, i, 0)),
                  _w_spec(w, k, tn, lambda bi, i, j: (layer, 0, j + off)),
                  pl.BlockSpec((SSD_CONV, tn), lambda bi, i, j: (0, j)),
                  pl.BlockSpec((1, tn), lambda bi, i, j: (0, j))],
        out_specs=pl.BlockSpec((None, tm, tn), lambda bi, i, j: (bi, i, j)),
        out_shape=jax.ShapeDtypeStruct((b, s, n), F32),
        scratch_shapes=[pltpu.VMEM((nj, SUBLANE, tn), F32)],
        compiler_params=_params(3, vmem),
        name="ssd_in_xbc_conv",
    )(h, w, conv_w, conv_b.reshape(1, n))


def _softplus_kernel(x_ref, b_ref, o_ref):
    v = x_ref[...] + b_ref[...]
    o_ref[...] = jnp.maximum(v, 0.0) + jnp.log1p(jnp.exp(-jnp.abs(v)))


def _ssd_dt(dt_raw, dt_bias_pad):
    b, s, n = dt_raw.shape
    tm = _pick(s, 1024)
    return pl.pallas_call(
        _softplus_kernel,
        grid=(b, s // tm),
        in_specs=[pl.BlockSpec((None, tm, n), lambda bi, i: (bi, i, 0)),
                  pl.BlockSpec((1, n), lambda bi, i: (0, 0))],
        out_specs=pl.BlockSpec((None, tm, n), lambda bi, i: (bi, i, 0)),
        out_shape=jax.ShapeDtypeStruct((b, s, n), F32),
        compiler_params=_params(2, 8 * tm * n * 4),
        name="ssd_dt",
    )(dt_raw, dt_bias_pad.reshape(1, n))


def _split3(a):
    hi = a.astype(BF16)
    r1 = a - hi.astype(F32)
    mid = r1.astype(BF16)
    lo = (r1 - mid.astype(F32)).astype(BF16)
    return hi, mid, lo


SSD_PIECES = 3
SSD_QW = SSD_PIECES * SSD_HPG
SSD_NCOPY = 3 * SSD_PIECES


def _ssd_expand_matrix():
    r = np.arange(LANE)[:, None]
    c = np.arange(3 * SSD_GW)[None, :]
    hit = (r < 3 * SSD_QW) & (c // SSD_GW == r // SSD_QW) & ((c % SSD_GW) // SSD_HEADDIM == r % SSD_HPG)
    return jnp.asarray(hit, dtype=BF16)


def _ssd_scan_kernel(x_ref, b_ref, c_ref, dtc_ref, dtr_ref, ac_ref, ar_ref, dsk_ref, ex_ref,
                     o_ref, st_scr):
    ci = pl.program_id(2)
    L = x_ref.shape[0]

    @pl.when(ci == 0)
    def _():
        st_scr[...] = jnp.zeros_like(st_scr)

    li = lax.broadcasted_iota(jnp.int32, (L, L), 0)
    si = lax.broadcasted_iota(jnp.int32, (L, L), 1)
    causal = li >= si
    tril = jnp.where(causal, 1.0, 0.0).astype(BF16)
    triu = jnp.where(li <= si, 1.0, 0.0).astype(BF16)
    pad = jnp.zeros((L, LANE - SSD_NCOPY * SSD_HPG), F32)
    lane = lax.broadcasted_iota(jnp.int32, (L, LANE), 1)
    piece = (lane % SSD_QW) // SSD_HPG
    left = lane < SSD_HEADDIM

    for gi in range(dtc_ref.shape[0]):
        gcols = slice(gi * SSD_GW, (gi + 1) * SSD_GW)
        ncols = slice(gi * SSD_STATE, (gi + 1) * SSD_STATE)

        dt_c = jnp.concatenate([dtc_ref[gi], pad], axis=1)
        a_c = jnp.concatenate([-jnp.exp(ac_ref[gi]), pad[0:1]], axis=1)
        acum = sum(_dot(tril, part) for part in _split3(dt_c * a_c))
        tot = acum[L - 1:L, :]
        quantity = jnp.where(lane < SSD_QW, dt_c,
                             jnp.where(lane < 2 * SSD_QW, jnp.exp(acum), jnp.exp(tot - acum)))
        hi, mid, lo = (p.astype(F32) for p in _split3(quantity))
        cols = jnp.where(piece == 0, hi, jnp.where(piece == 1, mid, lo)).astype(BF16)
        expanded = _dot(cols, ex_ref[...])
        dt_x = expanded[:, 0:SSD_GW]
        ea_x = expanded[:, SSD_GW:2 * SSD_GW]
        te_x = expanded[:, 2 * SSD_GW:3 * SSD_GW]

        acum_r = sum(_dot(part, triu) for part in _split3(dtr_ref[gi] * -jnp.exp(ar_ref[gi])))

        xs = x_ref[:, gcols]
        bm = b_ref[:, ncols]
        cm = c_ref[:, ncols].astype(BF16)
        cb = _dot_nt(cm, bm.astype(BF16))
        state = st_scr[gi]
        cs = _dot(cm, state.astype(BF16))
        bmt = jnp.transpose(bm).astype(BF16)
        xdt = xs * dt_x

        ys = []
        for t in range(SSD_GW // LANE):
            x2 = xdt[:, t * LANE:(t + 1) * LANE]
            y_t = None
            for h, keep in ((2 * t, left), (2 * t + 1, jnp.logical_not(left))):
                seg = acum[:, h:h + 1] - acum_r[h:h + 1, :]
                lmat = jnp.exp(jnp.where(causal, seg, NEG))
                part = _dot((cb * lmat).astype(BF16), jnp.where(keep, x2, 0.0).astype(BF16))
                y_t = part if y_t is None else y_t + part
            ys.append(y_t)
        y_diag = jnp.concatenate(ys, axis=1)
        o_ref[:, gcols] = y_diag + cs * ea_x + xs * dsk_ref[:, gcols]
        st_scr[gi] = state * ea_x[L - 1:L, :] + _dot(bmt, (xdt * te_x).astype(BF16))


def _ssd_scan(xbc, dtp, a_log, d_skip):
    b, s, _ = xbc.shape
    L = _pick(s, L_SSD)
    g = SSD_GROUPS
    gps = SSD_GROUPS_PER_STEP
    dt_g = dtp.reshape(b, s, g, SSD_HPG)
    dtc = jnp.tile(jnp.transpose(dt_g, (0, 2, 1, 3)), (1, 1, 1, SSD_NCOPY))
    dtr = jnp.transpose(dt_g, (0, 2, 3, 1))
    a_col = jnp.tile(a_log.reshape(g, 1, SSD_HPG), (1, 1, SSD_NCOPY))
    a_row = a_log.reshape(g, SSD_HPG, 1)
    ncol = SSD_NCOPY * SSD_HPG
    dsk = jnp.repeat(d_skip, SSD_HEADDIM).reshape(1, SSD_INNER)
    xw = gps * SSD_GW
    nw = gps * SSD_STATE
    boff = SSD_INNER // nw
    coff = boff + g // gps
    vmem = gps * (4 * L * SSD_GW * 4 + 24 * L * L * 4 + 16 * L * SSD_GW * 4)
    return pl.pallas_call(
        _ssd_scan_kernel,
        grid=(b, g // gps, s // L),
        in_specs=[pl.BlockSpec((None, L, xw), lambda bi, gi, c: (bi, c, gi)),
                  pl.BlockSpec((None, L, nw), lambda bi, gi, c: (bi, c, boff + gi)),
                  pl.BlockSpec((None, L, nw), lambda bi, gi, c: (bi, c, coff + gi)),
                  pl.BlockSpec((None, gps, L, ncol), lambda bi, gi, c: (bi, gi, c, 0)),
                  pl.BlockSpec((None, gps, SSD_HPG, L), lambda bi, gi, c: (bi, gi, 0, c)),
                  pl.BlockSpec((gps, 1, ncol), lambda bi, gi, c: (gi, 0, 0)),
                  pl.BlockSpec((gps, SSD_HPG, 1), lambda bi, gi, c: (gi, 0, 0)),
                  pl.BlockSpec((1, xw), lambda bi, gi, c: (0, gi)),
                  pl.BlockSpec((LANE, 3 * SSD_GW), lambda bi, gi, c: (0, 0))],
        out_specs=pl.BlockSpec((None, L, xw), lambda bi, gi, c: (bi, c, gi)),
        out_shape=jax.ShapeDtypeStruct((b, s, SSD_INNER), F32),
        scratch_shapes=[pltpu.VMEM((gps, SSD_STATE, SSD_GW), F32)],
        compiler_params=_params(3, vmem),
        name="ssd_scan",
    )(xbc, xbc, xbc, dtc, dtr, a_col, a_row, dsk, _ssd_expand_matrix())


def _gate_norm_kernel(y_ref, z_ref, g_ref, o_ref):
    v = y_ref[...] * _silu(z_ref[...].astype(F32))
    o_ref[...] = _rms_rows(v, g_ref[...]).astype(o_ref.dtype)


def _gate_norm(y, z, g):
    b, s, n = y.shape
    tm = _pick(s, 256)
    return pl.pallas_call(
        _gate_norm_kernel,
        grid=(b, s // tm),
        in_specs=[pl.BlockSpec((None, tm, n), lambda bi, i: (bi, i, 0)),
                  pl.BlockSpec((None, tm, n), lambda bi, i: (bi, i, 0)),
                  pl.BlockSpec((1, n), lambda bi, i: (0, 0))],
        out_specs=pl.BlockSpec((None, tm, n), lambda bi, i: (bi, i, 0)),
        out_shape=jax.ShapeDtypeStruct((b, s, n), BF16),
        compiler_params=_params(2, 8 * tm * n * 4),
        name="ssd_gate_norm",
    )(y, z, g.reshape(1, n))


def _final_norm_kernel(x_ref, g_ref, o_ref):
    o_ref[...] = _rms_rows(x_ref[...], g_ref[...])


def _final_norm(x, g):
    b, s, n = x.shape
    tm = _pick(s, 512)
    return pl.pallas_call(
        _final_norm_kernel,
        grid=(b, s // tm),
        in_specs=[pl.BlockSpec((None, tm, n), lambda bi, i: (bi, i, 0)),
                  pl.BlockSpec((1, n), lambda bi, i: (0, 0))],
        out_specs=pl.BlockSpec((None, tm, n), lambda bi, i: (bi, i, 0)),
        out_shape=jax.ShapeDtypeStruct((b, s, n), F32),
        compiler_params=_params(2, 8 * tm * n * 4),
        name="final_norm",
    )(x, g.reshape(1, n))


def _rope_tables(s):
    pos = jnp.arange(s, dtype=jnp.int32).astype(F32)[:, None]
    half = RET_DK // 2
    inv = RET_THETA ** (-jnp.arange(half, dtype=F32) / half)
    ang = pos * inv[None, :]
    ret_cos, ret_sin = jnp.cos(ang), jnp.sin(ang)
    hr = MLA_ROPE // 2
    inv_m = MLA_THETA ** (-jnp.arange(hr, dtype=F32) / hr)
    ang_m = pos * inv_m[None, :]
    cm, sm = jnp.cos(ang_m), jnp.sin(ang_m)
    pad = LANE - MLA_ROPE
    mla_cos = jnp.concatenate([cm, cm, jnp.ones((s, pad), F32)], axis=1)
    mla_sin = jnp.concatenate([-sm, sm, jnp.zeros((s, pad), F32)], axis=1)
    return ret_cos, ret_sin, mla_cos, mla_sin


def _hybrid_layer(x, mods, norm_g, w_in, layer, q_g, w_uq, kv_g, w_ukv, gn_g, w_out, tables):
    sh, sc, gt = mods
    ret_cos, ret_sin, mla_cos, mla_sin = tables
    o_v = 2 * RET_QK_W
    o_c = o_v + 2 * RET_V_W
    o_kr = o_c + 2 * MLA_RANK
    w_small = jnp.concatenate(
        [w_in[layer, :, o_c:o_kr + MLA_ROPE].astype(BF16), jnp.zeros((D_MODEL, LANE - MLA_ROPE), BF16)], axis=1)
    h = _norm_mod(x, norm_g, sh, sc)
    qk = _matmul(h, w_in, layer, 0, o_v, F32, "hyb_in_qk")
    vg = _matmul(h, w_in, layer, o_v, o_c - o_v, BF16, "hyb_in_vg")
    small = _matmul(h, w_small, None, 0, w_small.shape[1], F32, "hyb_in_latent")
    y_ret = _retention(qk, vg, ret_cos, ret_sin, gn_g)

    wq = w_uq.astype(BF16).reshape(MLA_RANK, MLA_HEADS, MLA_NOPE + MLA_ROPE)
    wq = jnp.pad(wq, ((0, 0), (0, 0), (0, MLA_QK_PAD - MLA_NOPE - MLA_ROPE)))
    wq = wq.reshape(MLA_RANK, MLA_HEADS * MLA_QK_PAD)
    wkv = w_ukv.astype(BF16).reshape(MLA_RANK, MLA_HEADS, MLA_NOPE + MLA_V)
    wk = wkv[:, :, :MLA_NOPE].reshape(MLA_RANK, MLA_HEADS * MLA_NOPE)
    wv = wkv[:, :, MLA_NOPE:].reshape(MLA_RANK, MLA_HEADS * MLA_V)
    q, k, v = _mla_up(small, q_g, kv_g, wq, wk, wv, mla_cos, mla_sin)
    y_mla = _mla_attention(q, k, v)
    return _matmul_res([y_ret, y_mla], w_out, layer, x, gt, "hyb_out")


def _ssd_layer(x, mods, norm_g, w_in, layer, conv_w, conv_b, dt_bias, a_log, d_skip, ssd_g, w_out):
    sh, sc, gt = mods
    o_x = SSD_INNER
    o_dt = o_x + SSD_CONV_DIM
    w_dt = jnp.concatenate(
        [w_in[layer, :, o_dt:].astype(BF16), jnp.zeros((D_MODEL, LANE - SSD_HEADS), BF16)], axis=1)
    h = _norm_mod(x, norm_g, sh, sc)
    z = _matmul(h, w_in, layer, 0, o_x, BF16, "ssd_in_z")
    xbc = _proj_conv(h, w_in, layer, o_x, conv_w, conv_b)
    dt_raw = _matmul(h, w_dt, None, 0, LANE, F32, "ssd_in_dt")
    dtp = _ssd_dt(dt_raw, jnp.pad(dt_bias, (0, LANE - SSD_HEADS)))[:, :, :SSD_HEADS]
    y = _ssd_scan(xbc, dtp, a_log, d_skip)
    hn = _gate_norm(y, z, ssd_g)
    return _matmul_res([hn], w_out, layer, x, gt, "ssd_out")


def _ffn_layer(x, mods, norm_g, w_up, w_down, layer, conv_w, conv_b):
    sh, sc, gt = mods
    a = _ffn_up(_norm_mod(x, norm_g, sh, sc), w_up, layer, conv_w, conv_b)
    return _matmul_res([a], w_down, layer, x, gt, "ffn_down")


def kernel(x, c, ada_w, ada_b, norm_mix_g, norm_ffn_g, hyb_w_in, hyb_q_norm_g, hyb_w_uq, hyb_kv_norm_g, hyb_w_ukv, hyb_ret_gn_g, hyb_w_out, ssd_w_in, ssd_conv_w, ssd_conv_b, ssd_dt_bias, ssd_a_log, ssd_d, ssd_norm_g, ssd_w_out, ffn_w_up, ffn_conv_w, ffn_conv_b, ffn_w_down, final_norm_g):
    b, s, d = x.shape
    depth = ada_w.shape[0]
    mods = _mods(c, ada_w, ada_b).reshape(depth, b, 6, 1, d)
    tables = _rope_tables(s)
    ffn_w_down = ffn_w_down.astype(BF16)
    for l in range(depth):
        m = [mods[l, :, k] for k in range(6)]
        i = l // 2
        if l % 2 == 0:
            x = _hybrid_layer(x, m[0:3], norm_mix_g[l], hyb_w_in, i, hyb_q_norm_g[i], hyb_w_uq[i],
                              hyb_kv_norm_g[i], hyb_w_ukv[i], hyb_ret_gn_g[i], hyb_w_out, tables)
        else:
            x = _ssd_layer(x, m[0:3], norm_mix_g[l], ssd_w_in, i, ssd_conv_w[i], ssd_conv_b[i],
                           ssd_dt_bias[i], ssd_a_log[i], ssd_d[i], ssd_norm_g[i], ssd_w_out)
        x = _ffn_layer(x, m[3:6], norm_ffn_g[l], ffn_w_up, ffn_w_down, l, ffn_conv_w[l], ffn_conv_b[l])
    return _final_norm(x, final_norm_g)
```

```python
import functools
import math

import numpy as np
import jax
import jax.numpy as jnp
from jax import lax
from jax.experimental import pallas as pl
from jax.experimental.pallas import tpu as pltpu

F32 = jnp.float32
BF16 = jnp.bfloat16

D_MODEL = 2048
DEPTH = 4
EPS = 1e-6

RET_HEADS = 4
RET_DK = 256
RET_DV = 512
RET_THETA = 10000.0
RET_QK_W = RET_HEADS * RET_DK
RET_V_W = RET_HEADS * RET_DV

MLA_HEADS = 16
MLA_NOPE = 128
MLA_ROPE = 64
MLA_V = 128
MLA_RANK = 512
MLA_THETA = 10000.0
MLA_QK_PAD = 256
MASK_CHUNK = 64

SSD_INNER = 4096
SSD_HEADDIM = 64
SSD_HEADS = 64
SSD_GROUPS = 8
SSD_HPG = SSD_HEADS // SSD_GROUPS
SSD_STATE = 128
SSD_CONV = 4
SSD_CONV_DIM = SSD_INNER + 2 * SSD_GROUPS * SSD_STATE
SSD_GW = SSD_HPG * SSD_HEADDIM

FFN_HIDDEN = 5632
FFN_CONV = 3

LANE = 128
SUBLANE = 8
VMEM_CAP = 56 * 1024 * 1024
NEG = -1e30

TM_PROJ = 1024
TM_MATMUL = 2048
TN_PROJ = 1024
TH_FFN = 512
FFN_ROW_CHUNK = 256
FFN_COL_CHUNK = 256
TM_NORM = 2048
NORM_ROW_CHUNK = 16
TN_RES = 512
L_RET = 256
L_SSD = 256
SSD_GROUPS_PER_STEP = 2
TQ_ATTN = 1024
TK_ATTN = 1024
TM_UP = 512


def _pick(n, pref):
    t = pref
    while t > 1 and n % t:
        t //= 2
    return t if n % t == 0 else n


def _params(n_axes, vmem_bytes):
    limit = int(min(VMEM_CAP, max(16 * 1024 * 1024, vmem_bytes * 1.3 + (4 << 20))))
    return pltpu.CompilerParams(dimension_semantics=("arbitrary",) * n_axes,
                                vmem_limit_bytes=limit)


def _silu(v):
    return v * (1.0 / (1.0 + jnp.exp(-v)))


def _dot(a, b):
    return jnp.dot(a, b, preferred_element_type=F32)


def _dot_nt(a, b):
    return lax.dot_general(a, b, (((1,), (1,)), ((), ())), preferred_element_type=F32)


def _shift_rows(g, k, prev8):
    c = g.shape[1]
    r = pltpu.roll(g, k, 0)
    hr = pltpu.roll(prev8, k, 0)
    rows = lax.broadcasted_iota(jnp.int32, (SUBLANE, c), 0)
    top = jnp.where(rows < k, hr, r[0:SUBLANE])
    if g.shape[0] == SUBLANE:
        return top
    return jnp.concatenate([top, r[SUBLANE:]], axis=0)


def _causal_conv(g, w_ref, b_ref, prev8):
    width = w_ref.shape[0]
    y = g * w_ref[width - 1:width, :] + b_ref[...]
    for k in range(1, width):
        y = y + _shift_rows(g, k, prev8) * w_ref[width - 1 - k:width - k, :]
    return y


def _mods_kernel(c_ref, w_ref, b_ref, o_ref):
    h = _silu(c_ref[...]).astype(BF16)
    o_ref[...] = _dot(h, w_ref[...].astype(BF16)) + b_ref[...]


def _mods(c, ada_w, ada_b):
    depth, d, n = ada_w.shape
    b = c.shape[0]
    tn = _pick(n, 1024)
    return pl.pallas_call(
        _mods_kernel,
        grid=(depth, n // tn),
        in_specs=[pl.BlockSpec((b, d), lambda l, j: (0, 0)),
                  pl.BlockSpec((None, d, tn), lambda l, j: (l, 0, j)),
                  pl.BlockSpec((None, 1, tn), lambda l, j: (l, 0, j))],
        out_specs=pl.BlockSpec((None, b, tn), lambda l, j: (l, 0, j)),
        out_shape=jax.ShapeDtypeStruct((depth, b, n), F32),
        compiler_params=_params(2, 2 * d * tn * 4 + d * tn * 2),
        name="adaln_mods",
    )(c, ada_w, ada_b.reshape(depth, 1, n))


def _norm_mod_kernel(x_ref, g_ref, sh_ref, sc_ref, o_ref):
    rc = NORM_ROW_CHUNK
    gs = g_ref[...] * (1.0 + sc_ref[...])
    sh = sh_ref[...]

    def body(r, carry):
        rows = pl.ds(pl.multiple_of(r * rc, rc), rc)
        xf = x_ref[rows, :]
        inv = lax.rsqrt(jnp.mean(xf * xf, axis=-1, keepdims=True) + EPS)
        o_ref[rows, :] = ((xf * inv) * gs + sh).astype(o_ref.dtype)
        return carry

    lax.fori_loop(0, x_ref.shape[0] // rc, body, 0, unroll=4)


def _norm_mod(x, g, shift, scale):
    b, s, d = x.shape
    tm = _pick(s, TM_NORM)
    return pl.pallas_call(
        _norm_mod_kernel,
        grid=(b, s // tm),
        in_specs=[pl.BlockSpec((None, tm, d), lambda bi, i: (bi, i, 0)),
                  pl.BlockSpec((1, d), lambda bi, i: (0, 0)),
                  pl.BlockSpec((None, 1, d), lambda bi, i: (bi, 0, 0)),
                  pl.BlockSpec((None, 1, d), lambda bi, i: (bi, 0, 0))],
        out_specs=pl.BlockSpec((None, tm, d), lambda bi, i: (bi, i, 0)),
        out_shape=jax.ShapeDtypeStruct((b, s, d), BF16),
        compiler_params=_params(2, 8 * tm * d * 4),
        name="norm_mod",
    )(x, g.reshape(1, d), shift, scale)


def _matmul_kernel(h_ref, w_ref, o_ref):
    o_ref[...] = _dot(h_ref[...], w_ref[...].astype(BF16)).astype(o_ref.dtype)


def _w_spec(w, rows, cols, index):
    if w.ndim == 2:
        return pl.BlockSpec((rows, cols), lambda bi, i, j: index(bi, i, j)[1:])
    return pl.BlockSpec((None, rows, cols), index)


def _matmul(h, w, layer, col0, n, out_dtype, name, tn=None):
    b, s, k = h.shape
    tm = _pick(s, TM_MATMUL if w.dtype == BF16 else TM_PROJ)
    if tn is None:
        tn = _pick(n, TN_PROJ) if n % TN_PROJ == 0 else n
    assert col0 % tn == 0
    off = col0 // tn
    osz = jnp.dtype(out_dtype).itemsize
    vmem = 2 * tm * k * 2 + 2 * k * tn * w.dtype.itemsize + k * tn * 2 + 2 * tm * tn * osz + tm * tn * 4
    return pl.pallas_call(
        _matmul_kernel,
        grid=(b, s // tm, n // tn),
        in_specs=[pl.BlockSpec((None, tm, k), lambda bi, i, j: (bi, i, 0)),
                  _w_spec(w, k, tn, lambda bi, i, j: (layer, 0, j + off))],
        out_specs=pl.BlockSpec((None, tm, tn), lambda bi, i, j: (bi, i, j)),
        out_shape=jax.ShapeDtypeStruct((b, s, n), out_dtype),
        compiler_params=_params(3, vmem),
        name=name,
    )(h, w)


def _ffn_up_kernel(h_ref, wu_ref, wg_ref, cw_ref, cb_ref, o_ref, halo_scr):
    si = pl.program_id(1)
    j = pl.program_id(2)
    tm, th = o_ref.shape
    rc = _pick(tm, FFN_ROW_CHUNK)
    cc = _pick(th, FFN_COL_CHUNK)
    for c in range(th // cc):
        cols = slice(c * cc, (c + 1) * cc)
        prev8 = jnp.where(si == 0, 0.0, halo_scr[j, :, cols])
        wu = wu_ref[:, cols].astype(BF16)
        wg = wg_ref[:, cols].astype(BF16)
        for r in range(tm // rc):
            rows = slice(r * rc, (r + 1) * rc)
            hr = h_ref[rows, :]
            u = _dot(hr, wu)
            gate = _dot(hr, wg)
            gc = _causal_conv(gate, cw_ref.at[:, cols], cb_ref.at[:, cols], prev8)
            o_ref[rows, cols] = (_silu(gc) * u).astype(o_ref.dtype)
            prev8 = gate[rc - SUBLANE:rc, :]
        halo_scr[j, :, cols] = prev8


def _ffn_up(h, w_up, layer, conv_w, conv_b):
    b, s, d = h.shape
    hid = conv_w.shape[1]
    tm = _pick(s, TM_MATMUL)
    th = _pick(hid, TH_FFN)
    nj = hid // th
    vmem = 2 * tm * d * 2 + 4 * d * th * w_up.dtype.itemsize + 2 * d * th * 2 + 2 * tm * th * 2 + 4 * tm * th * 4
    return pl.pallas_call(
        _ffn_up_kernel,
        grid=(b, s // tm, nj),
        in_specs=[pl.BlockSpec((None, tm, d), lambda bi, i, j: (bi, i, 0)),
                  _w_spec(w_up, d, th, lambda bi, i, j: (layer, 0, j)),
                  _w_spec(w_up, d, th, lambda bi, i, j: (layer, 0, j + nj)),
                  pl.BlockSpec((FFN_CONV, th), lambda bi, i, j: (0, j)),
                  pl.BlockSpec((1, th), lambda bi, i, j: (0, j))],
        out_specs=pl.BlockSpec((None, tm, th), lambda bi, i, j: (bi, i, j)),
        out_shape=jax.ShapeDtypeStruct((b, s, hid), BF16),
        scratch_shapes=[pltpu.VMEM((nj, SUBLANE, th), F32)],
        compiler_params=_params(3, vmem),
        name="ffn_up",
    )(h, w_up, w_up, conv_w, conv_b.reshape(1, hid))


def _matmul_res_kernel(*refs, n_a):
    a_refs = refs[:n_a]
    w_refs = refs[n_a:2 * n_a]
    x_ref, gt_ref, o_ref = refs[2 * n_a:]
    acc = _dot(a_refs[0][...], w_refs[0][...].astype(BF16))
    for a_ref, w_ref in zip(a_refs[1:], w_refs[1:]):
        acc = acc + _dot(a_ref[...], w_ref[...].astype(BF16))
    o_ref[...] = x_ref[...] + gt_ref[...] * acc


def _matmul_res(a_list, w, layer, x, gate, name):
    b, s, n = x.shape
    ks = [a.shape[2] for a in a_list]
    tm = _pick(s, TM_PROJ)
    tn = _pick(n, TN_RES)
    in_specs = [pl.BlockSpec((None, tm, k), lambda bi, i, j: (bi, i, 0)) for k in ks]
    assert len(set(ks)) == 1
    in_specs += [_w_spec(w, ks[0], tn, functools.partial(lambda bi, i, j, p: (layer, p, j), p=p))
                 for p in range(len(ks))]
    in_specs += [pl.BlockSpec((None, tm, tn), lambda bi, i, j: (bi, i, j)),
                 pl.BlockSpec((None, 1, tn), lambda bi, i, j: (bi, 0, j))]
    ktot = sum(ks)
    vmem = 2 * tm * ktot * 2 + 2 * ktot * tn * w.dtype.itemsize + ktot * tn * 2 + 5 * tm * tn * 4
    return pl.pallas_call(
        functools.partial(_matmul_res_kernel, n_a=len(a_list)),
        grid=(b, s // tm, n // tn),
        in_specs=in_specs,
        out_specs=pl.BlockSpec((None, tm, tn), lambda bi, i, j: (bi, i, j)),
        out_shape=jax.ShapeDtypeStruct((b, s, n), F32),
        compiler_params=_params(3, vmem),
        name=name,
    )(*a_list, *([w] * len(a_list)), x, gate)


def _ret_log_gamma(h):
    return math.log1p(-(2.0 ** (-5.0 - h)))


def _retention_kernel(q_ref, k_ref, v_ref, g_ref, cos_ref, sin_ref, gn_ref, o_ref, r_scr):
    ci = pl.program_id(1)
    L = q_ref.shape[0]

    @pl.when(ci == 0)
    def _():
        r_scr[...] = jnp.zeros_like(r_scr)

    cos = cos_ref[...]
    sin = sin_ref[...]
    li = lax.broadcasted_iota(jnp.int32, (L, L), 0)
    si = lax.broadcasted_iota(jnp.int32, (L, L), 1)
    rel = (li - si).astype(F32)
    causal = li >= si
    pos = lax.broadcasted_iota(jnp.int32, (L, 1), 0).astype(F32)
    half = RET_DK // 2

    def rope(t):
        t1 = t[:, :half]
        t2 = t[:, half:]
        return jnp.concatenate([t1 * cos - t2 * sin, t1 * sin + t2 * cos], axis=1)

    for h in range(RET_HEADS):
        lg = _ret_log_gamma(h)
        q = rope(q_ref[:, h * RET_DK:(h + 1) * RET_DK])
        k = rope(k_ref[:, h * RET_DK:(h + 1) * RET_DK]) * (RET_DK ** -0.5)
        v = v_ref[:, h * RET_DV:(h + 1) * RET_DV].astype(F32)
        qb = q.astype(BF16)
        decay = jnp.where(causal, jnp.exp(jnp.where(causal, rel, 0.0) * lg), 0.0)
        inner = _dot_nt(qb, k.astype(BF16)) * decay
        xi = jnp.exp(lg * (pos + 1.0))
        zeta = jnp.exp(lg * (L - 1.0 - pos))
        r_old = r_scr[h]
        y = _dot(inner.astype(BF16), v.astype(BF16)) + _dot(qb, r_old.astype(BF16)) * xi
        kt = jnp.transpose(k).astype(BF16)
        r_scr[h] = math.exp(lg * L) * r_old + _dot(kt, (v * zeta).astype(BF16))
        yc = y - jnp.mean(y, axis=-1, keepdims=True)
        yn = yc * lax.rsqrt(jnp.mean(yc * yc, axis=-1, keepdims=True) + EPS)
        cols = slice(h * RET_DV, (h + 1) * RET_DV)
        gate = _silu(g_ref[:, cols].astype(F32))
        o_ref[:, cols] = (gate * (yn * gn_ref[:, cols])).astype(o_ref.dtype)


def _retention(qk, vg, cos, sin, gn_g):
    b, s, _ = qk.shape
    L = _pick(s, L_RET)
    vmem = 2 * (2 * L * RET_QK_W * 4 + 2 * L * RET_V_W * 2 + L * RET_V_W * 2) \
        + RET_HEADS * RET_DK * RET_DV * 4 + 24 * L * RET_DV * 4
    return pl.pallas_call(
        _retention_kernel,
        grid=(b, s // L),
        in_specs=[pl.BlockSpec((None, L, RET_QK_W), lambda bi, c: (bi, c, 0)),
                  pl.BlockSpec((None, L, RET_QK_W), lambda bi, c: (bi, c, 1)),
                  pl.BlockSpec((None, L, RET_V_W), lambda bi, c: (bi, c, 0)),
                  pl.BlockSpec((None, L, RET_V_W), lambda bi, c: (bi, c, 1)),
                  pl.BlockSpec((L, RET_DK // 2), lambda bi, c: (c, 0)),
                  pl.BlockSpec((L, RET_DK // 2), lambda bi, c: (c, 0)),
                  pl.BlockSpec((1, RET_V_W), lambda bi, c: (0, 0))],
        out_specs=pl.BlockSpec((None, L, RET_V_W), lambda bi, c: (bi, c, 0)),
        out_shape=jax.ShapeDtypeStruct((b, s, RET_V_W), BF16),
        scratch_shapes=[pltpu.VMEM((RET_HEADS, RET_DK, RET_DV), F32)],
        compiler_params=_params(2, vmem),
        name="retention",
    )(qk, qk, vg, vg, cos, sin, gn_g.reshape(1, RET_V_W))


def _rms_rows(x, g):
    return x * lax.rsqrt(jnp.mean(x * x, axis=-1, keepdims=True) + EPS) * g


def _rope_slot(t, cos_t, sin_t):
    lane = lax.broadcasted_iota(jnp.int32, t.shape, 1)
    hr = MLA_ROPE // 2
    swapped = jnp.where(lane < hr, pltpu.roll(t, LANE - hr, 1), pltpu.roll(t, hr, 1))
    return t * cos_t + swapped * sin_t


def _q_up_kernel(c_ref, g_ref, w_ref, cos_ref, sin_ref, o_ref):
    h = _rms_rows(c_ref[...], g_ref[...]).astype(BF16)
    q = _dot(h, w_ref[...]) * MLA_Q_SCALE
    cos_t = cos_ref[...]
    sin_t = sin_ref[...]
    for hd in range(MLA_HEADS):
        base = hd * MLA_QK_PAD
        o_ref[:, base:base + MLA_NOPE] = q[:, base:base + MLA_NOPE].astype(o_ref.dtype)
        rs = q[:, base + MLA_NOPE:base + MLA_QK_PAD]
        o_ref[:, base + MLA_NOPE:base + MLA_QK_PAD] = _rope_slot(rs, cos_t, sin_t).astype(o_ref.dtype)


def _kv_up_kernel(c_ref, kr_ref, g_ref, wk_ref, wv_ref, cos_ref, sin_ref, k_ref, v_ref):
    h = _rms_rows(c_ref[...], g_ref[...]).astype(BF16)
    kn = _dot(h, wk_ref[...])
    v_ref[...] = _dot(h, wv_ref[...]).astype(v_ref.dtype)
    lane = lax.broadcasted_iota(jnp.int32, kr_ref.shape, 1)
    kr = jnp.where(lane < MLA_ROPE, kr_ref[...], 0.0)
    kr = _rope_slot(kr, cos_ref[...], sin_ref[...]).astype(k_ref.dtype)
    for hd in range(MLA_HEADS):
        base = hd * MLA_QK_PAD
        k_ref[:, base:base + MLA_NOPE] = kn[:, hd * MLA_NOPE:(hd + 1) * MLA_NOPE].astype(k_ref.dtype)
        k_ref[:, base + MLA_NOPE:base + MLA_QK_PAD] = kr


def _mla_up(small, q_g, kv_g, wq, wk, wv, cos_t, sin_t):
    b, s, _ = small.shape
    tm = _pick(s, TM_UP)
    nq = MLA_HEADS * MLA_QK_PAD
    nv = MLA_HEADS * MLA_V
    tab = pl.BlockSpec((tm, LANE), lambda bi, i: (i, 0))
    q = pl.pallas_call(
        _q_up_kernel,
        grid=(b, s // tm),
        in_specs=[pl.BlockSpec((None, tm, MLA_RANK), lambda bi, i: (bi, i, 0)),
                  pl.BlockSpec((1, MLA_RANK), lambda bi, i: (0, 0)),
                  pl.BlockSpec((MLA_RANK, nq), lambda bi, i: (0, 0)),
                  tab, tab],
        out_specs=pl.BlockSpec((None, tm, nq), lambda bi, i: (bi, i, 0)),
        out_shape=jax.ShapeDtypeStruct((b, s, nq), BF16),
        compiler_params=_params(2, 2 * MLA_RANK * nq * 2 + 2 * tm * nq * 2 + 3 * tm * nq * 4),
        name="mla_q_up",
    )(small, q_g.reshape(1, MLA_RANK), wq, cos_t, sin_t)
    k, v = pl.pallas_call(
        _kv_up_kernel,
        grid=(b, s // tm),
        in_specs=[pl.BlockSpec((None, tm, MLA_RANK), lambda bi, i: (bi, i, 1)),
                  pl.BlockSpec((None, tm, LANE), lambda bi, i: (bi, i, 2 * MLA_RANK // LANE)),
                  pl.BlockSpec((1, MLA_RANK), lambda bi, i: (0, 0)),
                  pl.BlockSpec((MLA_RANK, nv), lambda bi, i: (0, 0)),
                  pl.BlockSpec((MLA_RANK, nv), lambda bi, i: (0, 0)),
                  tab, tab],
        out_specs=[pl.BlockSpec((None, tm, nq), lambda bi, i: (bi, i, 0)),
                   pl.BlockSpec((None, tm, nv), lambda bi, i: (bi, i, 0))],
        out_shape=[jax.ShapeDtypeStruct((b, s, nq), BF16),
                   jax.ShapeDtypeStruct((b, s, nv), BF16)],
        compiler_params=_params(2, 4 * MLA_RANK * nv * 2 + 2 * tm * (nq + nv) * 2 + 3 * tm * nq * 4),
        name="mla_kv_up",
    )(small, small, kv_g.reshape(1, MLA_RANK), wk, wv, cos_t, sin_t)
    return q, k, v


MLA_Q_SCALE = ((MLA_NOPE + MLA_ROPE) ** -0.5) * math.log2(math.e)


def _mla_attn_kernel(q_ref, k_ref, v_ref, o_ref, *, tk):
    qi = pl.program_id(2)
    tq = q_ref.shape[0]
    q = q_ref[...]

    def update(carry, s2, v):
        m, l, acc = carry
        m_new = jnp.maximum(m, jnp.max(s2, axis=-1, keepdims=True))
        p = jnp.exp2(s2 - m_new)
        alpha = jnp.exp2(m - m_new)
        l = alpha * l + jnp.sum(p, axis=-1, keepdims=True)
        acc = alpha * acc + _dot(p.astype(BF16), v)
        return m_new, l, acc

    def body(ki, carry):
        rows = pl.ds(pl.multiple_of(ki * tk, tk), tk)
        return update(carry, _dot_nt(q, k_ref[rows, :]), v_ref[rows, :])

    init = (jnp.full((tq, 1), NEG, F32), jnp.zeros((tq, 1), F32), jnp.zeros((tq, MLA_V), F32))
    carry = lax.fori_loop(0, qi * (tq // tk), body, init)

    qc = lax.broadcasted_iota(jnp.int32, (tq, tk), 0) // MASK_CHUNK
    kc = lax.broadcasted_iota(jnp.int32, (tq, tk), 1) // MASK_CHUNK
    for j in range(tq // tk):
        rows = pl.ds(pl.multiple_of(qi * tq + j * tk, tk), tk)
        s2 = jnp.where(kc + (j * tk) // MASK_CHUNK <= qc, _dot_nt(q, k_ref[rows, :]), NEG)
        carry = update(carry, s2, v_ref[rows, :])
    m, l, acc = carry
    o_ref[...] = (acc / l).astype(o_ref.dtype)


def _mla_attention(q, k, v):
    b, s, _ = q.shape
    tq = _pick(s, TQ_ATTN)
    tk = _pick(tq, TK_ATTN)
    vmem = 2 * (tq * MLA_QK_PAD * 2 + s * MLA_QK_PAD * 2 + s * MLA_V * 2 + tq * MLA_V * 2) \
        + 6 * tq * tk * 4
    return pl.pallas_call(
        functools.partial(_mla_attn_kernel, tk=tk),
        grid=(b, MLA_HEADS, s // tq),
        in_specs=[pl.BlockSpec((None, tq, MLA_QK_PAD), lambda bi, h, i: (bi, i, h)),
                  pl.BlockSpec((None, s, MLA_QK_PAD), lambda bi, h, i: (bi, 0, h)),
                  pl.BlockSpec((None, s, MLA_V), lambda bi, h, i: (bi, 0, h))],
        out_specs=pl.BlockSpec((None, tq, MLA_V), lambda bi, h, i: (bi, i, h)),
        out_shape=jax.ShapeDtypeStruct((b, s, MLA_HEADS * MLA_V), BF16),
        compiler_params=_params(3, vmem),
        name="mla_attention",
    )(q, k, v)


def _proj_conv_kernel(h_ref, w_ref, cw_ref, cb_ref, o_ref, halo_scr):
    si = pl.program_id(1)
    j = pl.program_id(2)
    tm, tn = o_ref.shape
    rc = _pick(tm, FFN_ROW_CHUNK)
    cc = _pick(tn, FFN_COL_CHUNK)
    for c in range(tn // cc):
        cols = slice(c * cc, (c + 1) * cc)
        prev8 = jnp.where(si == 0, 0.0, halo_scr[j, :, cols])
        wc = w_ref[:, cols].astype(BF16)
        for r in range(tm // rc):
            rows = slice(r * rc, (r + 1) * rc)
            raw = _dot(h_ref[rows, :], wc)
            o_ref[rows, cols] = _silu(_causal_conv(raw, cw_ref.at[:, cols], cb_ref.at[:, cols], prev8))
            prev8 = raw[rc - SUBLANE:rc, :]
        halo_scr[j, :, cols] = prev8


def _proj_conv(h, w, layer, col0, conv_w, conv_b):
    b, s, k = h.shape
    n = conv_w.shape[1]
    tm = _pick(s, TM_PROJ)
    tn = _pick(n, TN_PROJ)
    nj = n // tn
    assert col0 % tn == 0
    off = col0 // tn
    vmem = 2 * tm * k * 2 + 2 * k * tn * w.dtype.itemsize + k * tn * 2 + 2 * tm * tn * 4 + 4 * tm * tn * 4
    return pl.pallas_call(
        _proj_conv_kernel,
        grid=(b, s // tm, nj),
        in_specs=[pl.BlockSpec((None, tm, k), lambda bi, i, j: (bi, i, 0)),
                  _w_spec(w, k, tn, lambda bi, i, j: (layer, 0, j + off)),
                  pl.BlockSpec((SSD_CONV, tn), lambda bi, i, j: (0, j)),
                  pl.BlockSpec((1, tn), lambda bi, i, j: (0, j))],
        out_specs=pl.BlockSpec((None, tm, tn), lambda bi, i, j: (bi, i, j)),
        out_shape=jax.ShapeDtypeStruct((b, s, n), F32),
        scratch_shapes=[pltpu.VMEM((nj, SUBLANE, tn), F32)],
        compiler_params=_params(3, vmem),
        name="ssd_in_xbc_conv",
    )(h, w, conv_w, conv_b.reshape(1, n))


def _softplus_kernel(x_ref, b_ref, o_ref):
    v = x_ref[...] + b_ref[...]
    o_ref[...] = jnp.maximum(v, 0.0) + jnp.log1p(jnp.exp(-jnp.abs(v)))


def _ssd_dt(dt_raw, dt_bias_pad):
    b, s, n = dt_raw.shape
    tm = _pick(s, 1024)
    return pl.pallas_call(
        _softplus_kernel,
        grid=(b, s // tm),
        in_specs=[pl.BlockSpec((None, tm, n), lambda bi, i: (bi, i, 0)),
                  pl.BlockSpec((1, n), lambda bi, i: (0, 0))],
        out_specs=pl.BlockSpec((None, tm, n), lambda bi, i: (bi, i, 0)),
        out_shape=jax.ShapeDtypeStruct((b, s, n), F32),
        compiler_params=_params(2, 8 * tm * n * 4),
        name="ssd_dt",
    )(dt_raw, dt_bias_pad.reshape(1, n))


def _split3(a):
    hi = a.astype(BF16)
    r1 = a - hi.astype(F32)
    mid = r1.astype(BF16)
    lo = (r1 - mid.astype(F32)).astype(BF16)
    return hi, mid, lo


SSD_PIECES = 3
SSD_QW = SSD_PIECES * SSD_HPG
SSD_NCOPY = 3 * SSD_PIECES


def _ssd_expand_matrix():
    r = np.arange(LANE)[:, None]
    c = np.arange(3 * SSD_GW)[None, :]
    hit = (r < 3 * SSD_QW) & (c // SSD_GW == r // SSD_QW) & ((c % SSD_GW) // SSD_HEADDIM == r % SSD_HPG)
    return jnp.asarray(hit, dtype=BF16)


def _ssd_scan_kernel(x_ref, b_ref, c_ref, dtc_ref, dtr_ref, ac_ref, ar_ref, dsk_ref, ex_ref,
                     o_ref, st_scr):
    ci = pl.program_id(2)
    L = x_ref.shape[0]

    @pl.when(ci == 0)
    def _():
        st_scr[...] = jnp.zeros_like(st_scr)

    li = lax.broadcasted_iota(jnp.int32, (L, L), 0)
    si = lax.broadcasted_iota(jnp.int32, (L, L), 1)
    causal = li >= si
    tril = jnp.where(causal, 1.0, 0.0).astype(BF16)
    triu = jnp.where(li <= si, 1.0, 0.0).astype(BF16)
    lane = lax.broadcasted_iota(jnp.int32, (L, LANE), 1)
    piece = (lane % SSD_QW) // SSD_HPG
    left = lane < SSD_HEADDIM

    for gi in range(dtr_ref.shape[0]):
        gcols = slice(gi * SSD_GW, (gi + 1) * SSD_GW)
        ncols = slice(gi * SSD_STATE, (gi + 1) * SSD_STATE)
        lcols = slice(gi * LANE, (gi + 1) * LANE)

        dt_c = dtc_ref[:, lcols]
        a_c = -jnp.exp(ac_ref[:, lcols])
        acum = sum(_dot(tril, part) for part in _split3(dt_c * a_c))
        tot = acum[L - 1:L, :]
        quantity = jnp.where(lane < SSD_QW, dt_c,
                             jnp.where(lane < 2 * SSD_QW, jnp.exp(acum), jnp.exp(tot - acum)))
        hi, mid, lo = (p.astype(F32) for p in _split3(quantity))
        cols = jnp.where(piece == 0, hi, jnp.where(piece == 1, mid, lo)).astype(BF16)
        expanded = _dot(cols, ex_ref[...])
        dt_x = expanded[:, 0:SSD_GW]
        ea_x = expanded[:, SSD_GW:2 * SSD_GW]
        te_x = expanded[:, 2 * SSD_GW:3 * SSD_GW]

        acum_r = sum(_dot(part, triu) for part in _split3(dtr_ref[gi] * -jnp.exp(ar_ref[gi])))

        xs = x_ref[:, gcols]
        bm = b_ref[:, ncols]
        cm = c_ref[:, ncols].astype(BF16)
        cb = _dot_nt(cm, bm.astype(BF16))
        state = st_scr[gi]
        cs = _dot(cm, state.astype(BF16))
        bmt = jnp.transpose(bm).astype(BF16)
        xdt = xs * dt_x

        ys = []
        for t in range(SSD_GW // LANE):
            x2 = xdt[:, t * LANE:(t + 1) * LANE]
            y_t = None
            for h, keep in ((2 * t, left), (2 * t + 1, jnp.logical_not(left))):
                seg = acum[:, h:h + 1] - acum_r[h:h + 1, :]
                lmat = jnp.exp(jnp.where(causal, seg, NEG))
                part = _dot((cb * lmat).astype(BF16), jnp.where(keep, x2, 0.0).astype(BF16))
                y_t = part if y_t is None else y_t + part
            ys.append(y_t)
        y_diag = jnp.concatenate(ys, axis=1)
        o_ref[:, gcols] = y_diag + cs * ea_x + xs * dsk_ref[:, gcols]
        st_scr[gi] = state * ea_x[L - 1:L, :] + _dot(bmt, (xdt * te_x).astype(BF16))


def _ssd_scan(xbc, dtp, a_log, d_skip):
    b, s, _ = xbc.shape
    L = _pick(s, L_SSD)
    g = SSD_GROUPS
    gps = SSD_GROUPS_PER_STEP
    dt_g = dtp.reshape(b, s, g, SSD_HPG)
    lane_pad = LANE - SSD_NCOPY * SSD_HPG

    def group_lanes(t):
        t = jnp.pad(jnp.tile(t, SSD_NCOPY), [(0, 0)] * (t.ndim - 1) + [(0, lane_pad)])
        return t.reshape(t.shape[:-2] + (g * LANE,))

    dtc = group_lanes(dt_g)
    dtr = jnp.transpose(dt_g, (0, 2, 3, 1))
    a_col = group_lanes(a_log.reshape(1, g, SSD_HPG))
    a_row = a_log.reshape(g, SSD_HPG, 1)
    dsk = jnp.repeat(d_skip, SSD_HEADDIM).reshape(1, SSD_INNER)
    xw = gps * SSD_GW
    nw = gps * SSD_STATE
    boff = SSD_INNER // nw
    coff = boff + g // gps
    vmem = gps * (4 * L * SSD_GW * 4 + 24 * L * L * 4 + 16 * L * SSD_GW * 4)
    return pl.pallas_call(
        _ssd_scan_kernel,
        grid=(b, g // gps, s // L),
        in_specs=[pl.BlockSpec((None, L, xw), lambda bi, gi, c: (bi, c, gi)),
                  pl.BlockSpec((None, L, nw), lambda bi, gi, c: (bi, c, boff + gi)),
                  pl.BlockSpec((None, L, nw), lambda bi, gi, c: (bi, c, coff + gi)),
                  pl.BlockSpec((None, L, gps * LANE), lambda bi, gi, c: (bi, c, gi)),
                  pl.BlockSpec((None, gps, SSD_HPG, L), lambda bi, gi, c: (bi, gi, 0, c)),
                  pl.BlockSpec((1, gps * LANE), lambda bi, gi, c: (0, gi)),
                  pl.BlockSpec((gps, SSD_HPG, 1), lambda bi, gi, c: (gi, 0, 0)),
                  pl.BlockSpec((1, xw), lambda bi, gi, c: (0, gi)),
                  pl.BlockSpec((LANE, 3 * SSD_GW), lambda bi, gi, c: (0, 0))],
        out_specs=pl.BlockSpec((None, L, xw), lambda bi, gi, c: (bi, c, gi)),
        out_shape=jax.ShapeDtypeStruct((b, s, SSD_INNER), F32),
        scratch_shapes=[pltpu.VMEM((gps, SSD_STATE, SSD_GW), F32)],
        compiler_params=_params(3, vmem),
        name="ssd_scan",
    )(xbc, xbc, xbc, dtc, dtr, a_col, a_row, dsk, _ssd_expand_matrix())


def _gate_norm_kernel(y_ref, z_ref, g_ref, o_ref):
    v = y_ref[...] * _silu(z_ref[...].astype(F32))
    o_ref[...] = _rms_rows(v, g_ref[...]).astype(o_ref.dtype)


def _gate_norm(y, z, g):
    b, s, n = y.shape
    tm = _pick(s, 256)
    return pl.pallas_call(
        _gate_norm_kernel,
        grid=(b, s // tm),
        in_specs=[pl.BlockSpec((None, tm, n), lambda bi, i: (bi, i, 0)),
                  pl.BlockSpec((None, tm, n), lambda bi, i: (bi, i, 0)),
                  pl.BlockSpec((1, n), lambda bi, i: (0, 0))],
        out_specs=pl.BlockSpec((None, tm, n), lambda bi, i: (bi, i, 0)),
        out_shape=jax.ShapeDtypeStruct((b, s, n), BF16),
        compiler_params=_params(2, 8 * tm * n * 4),
        name="ssd_gate_norm",
    )(y, z, g.reshape(1, n))


def _final_norm_kernel(x_ref, g_ref, o_ref):
    o_ref[...] = _rms_rows(x_ref[...], g_ref[...])


def _final_norm(x, g):
    b, s, n = x.shape
    tm = _pick(s, 512)
    return pl.pallas_call(
        _final_norm_kernel,
        grid=(b, s // tm),
        in_specs=[pl.BlockSpec((None, tm, n), lambda bi, i: (bi, i, 0)),
                  pl.BlockSpec((1, n), lambda bi, i: (0, 0))],
        out_specs=pl.BlockSpec((None, tm, n), lambda bi, i: (bi, i, 0)),
        out_shape=jax.ShapeDtypeStruct((b, s, n), F32),
        compiler_params=_params(2, 8 * tm * n * 4),
        name="final_norm",
    )(x, g.reshape(1, n))


def _rope_tables(s):
    pos = jnp.arange(s, dtype=jnp.int32).astype(F32)[:, None]
    half = RET_DK // 2
    inv = RET_THETA ** (-jnp.arange(half, dtype=F32) / half)
    ang = pos * inv[None, :]
    ret_cos, ret_sin = jnp.cos(ang), jnp.sin(ang)
    hr = MLA_ROPE // 2
    inv_m = MLA_THETA ** (-jnp.arange(hr, dtype=F32) / hr)
    ang_m = pos * inv_m[None, :]
    cm, sm = jnp.cos(ang_m), jnp.sin(ang_m)
    pad = LANE - MLA_ROPE
    mla_cos = jnp.concatenate([cm, cm, jnp.ones((s, pad), F32)], axis=1)
    mla_sin = jnp.concatenate([-sm, sm, jnp.zeros((s, pad), F32)], axis=1)
    return ret_cos, ret_sin, mla_cos, mla_sin


def _hybrid_layer(x, mods, norm_g, w_in, layer, q_g, w_uq, kv_g, w_ukv, gn_g, w_out, tables):
    sh, sc, gt = mods
    ret_cos, ret_sin, mla_cos, mla_sin = tables
    o_v = 2 * RET_QK_W
    o_c = o_v + 2 * RET_V_W
    o_kr = o_c + 2 * MLA_RANK
    h = _norm_mod(x, norm_g, sh, sc)
    qk = _matmul(h, w_in, layer, 0, o_v, F32, "hyb_in_qk")
    vg = _matmul(h, w_in, layer, o_v, o_c - o_v, BF16, "hyb_in_vg")
    small = _matmul(h, w_in, layer, o_c, o_kr - o_c + LANE, F32, "hyb_in_latent", tn=3 * LANE)
    y_ret = _retention(qk, vg, ret_cos, ret_sin, gn_g)

    wq = w_uq.astype(BF16).reshape(MLA_RANK, MLA_HEADS, MLA_NOPE + MLA_ROPE)
    wq = jnp.pad(wq, ((0, 0), (0, 0), (0, MLA_QK_PAD - MLA_NOPE - MLA_ROPE)))
    wq = wq.reshape(MLA_RANK, MLA_HEADS * MLA_QK_PAD)
    wkv = w_ukv.astype(BF16).reshape(MLA_RANK, MLA_HEADS, MLA_NOPE + MLA_V)
    wk = wkv[:, :, :MLA_NOPE].reshape(MLA_RANK, MLA_HEADS * MLA_NOPE)
    wv = wkv[:, :, MLA_NOPE:].reshape(MLA_RANK, MLA_HEADS * MLA_V)
    q, k, v = _mla_up(small, q_g, kv_g, wq, wk, wv, mla_cos, mla_sin)
    y_mla = _mla_attention(q, k, v)
    return _matmul_res([y_ret, y_mla], w_out, layer, x, gt, "hyb_out")


def _ssd_layer(x, mods, norm_g, w_in, layer, conv_w, conv_b, dt_bias, a_log, d_skip, ssd_g, w_out):
    sh, sc, gt = mods
    o_x = SSD_INNER
    o_dt = o_x + SSD_CONV_DIM
    h = _norm_mod(x, norm_g, sh, sc)
    z = _matmul(h, w_in, layer, 0, o_x, BF16, "ssd_in_z")
    xbc = _proj_conv(h, w_in, layer, o_x, conv_w, conv_b)
    dt_raw = _matmul(h, w_in, layer, o_dt, LANE, F32, "ssd_in_dt")
    dtp = _ssd_dt(dt_raw, jnp.pad(dt_bias, (0, LANE - SSD_HEADS)))[:, :, :SSD_HEADS]
    y = _ssd_scan(xbc, dtp, a_log, d_skip)
    hn = _gate_norm(y, z, ssd_g)
    return _matmul_res([hn], w_out, layer, x, gt, "ssd_out")


def _ffn_layer(x, mods, norm_g, w_up, w_down, layer, conv_w, conv_b):
    sh, sc, gt = mods
    a = _ffn_up(_norm_mod(x, norm_g, sh, sc), w_up, layer, conv_w, conv_b)
    return _matmul_res([a], w_down, layer, x, gt, "ffn_down")


def kernel(x, c, ada_w, ada_b, norm_mix_g, norm_ffn_g, hyb_w_in, hyb_q_norm_g, hyb_w_uq, hyb_kv_norm_g, hyb_w_ukv, hyb_ret_gn_g, hyb_w_out, ssd_w_in, ssd_conv_w, ssd_conv_b, ssd_dt_bias, ssd_a_log, ssd_d, ssd_norm_g, ssd_w_out, ffn_w_up, ffn_conv_w, ffn_conv_b, ffn_w_down, final_norm_g):
    b, s, d = x.shape
    depth = ada_w.shape[0]
    mods = _mods(c, ada_w, ada_b).reshape(depth, b, 6, 1, d)
    tables = _rope_tables(s)
    hyb_w_in, hyb_w_out, ssd_w_in, ssd_w_out, ffn_w_down = (
        w.astype(BF16) for w in (hyb_w_in, hyb_w_out, ssd_w_in, ssd_w_out, ffn_w_down))
    for l in range(depth):
        m = [mods[l, :, k] for k in range(6)]
        i = l // 2
        if l % 2 == 0:
            x = _hybrid_layer(x, m[0:3], norm_mix_g[l], hyb_w_in, i, hyb_q_norm_g[i], hyb_w_uq[i],
                              hyb_kv_norm_g[i], hyb_w_ukv[i], hyb_ret_gn_g[i], hyb_w_out, tables)
        else:
            x = _ssd_layer(x, m[0:3], norm_mix_g[l], ssd_w_in, i, ssd_conv_w[i], ssd_conv_b[i],
                           ssd_dt_bias[i], ssd_a_log[i], ssd_d[i], ssd_norm_g[i], ssd_w_out)
        x = _ffn_layer(x, m[3:6], norm_ffn_g[l], ffn_w_up, ffn_w_down, l, ffn_conv_w[l], ffn_conv_b[l])
    return _final_norm(x, final_norm_g)
```

```python
import functools
import math

import numpy as np
import jax
import jax.numpy as jnp
from jax import lax
from jax.experimental import pallas as pl
from jax.experimental.pallas import tpu as pltpu

F32 = jnp.float32
BF16 = jnp.bfloat16

D_MODEL = 2048
DEPTH = 4
EPS = 1e-6

RET_HEADS = 4
RET_DK = 256
RET_DV = 512
RET_THETA = 10000.0
RET_QK_W = RET_HEADS * RET_DK
RET_V_W = RET_HEADS * RET_DV

MLA_HEADS = 16
MLA_NOPE = 128
MLA_ROPE = 64
MLA_V = 128
MLA_RANK = 512
MLA_THETA = 10000.0
MLA_QK_PAD = 256
MASK_CHUNK = 64

SSD_INNER = 4096
SSD_HEADDIM = 64
SSD_HEADS = 64
SSD_GROUPS = 8
SSD_HPG = SSD_HEADS // SSD_GROUPS
SSD_STATE = 128
SSD_CONV = 4
SSD_CONV_DIM = SSD_INNER + 2 * SSD_GROUPS * SSD_STATE
SSD_GW = SSD_HPG * SSD_HEADDIM

FFN_HIDDEN = 5632
FFN_CONV = 3

LANE = 128
SUBLANE = 8
VMEM_CAP = 56 * 1024 * 1024
NEG = -1e30

TM_PROJ = 1024
TM_MATMUL = 2048
TN_PROJ = 1024
TH_FFN = 512
FFN_ROW_CHUNK = 256
FFN_COL_CHUNK = 256
TM_NORM = 2048
NORM_ROW_CHUNK = 16
TN_RES = 512
L_RET = 256
L_SSD = 256
SSD_GROUPS_PER_STEP = 2
TQ_ATTN = 1024
TK_ATTN = 1024
TM_UP = 512


def _pick(n, pref):
    t = pref
    while t > 1 and n % t:
        t //= 2
    return t if n % t == 0 else n


def _params(n_axes, vmem_bytes):
    limit = int(min(VMEM_CAP, max(16 * 1024 * 1024, vmem_bytes * 1.3 + (4 << 20))))
    return pltpu.CompilerParams(dimension_semantics=("arbitrary",) * n_axes,
                                vmem_limit_bytes=limit)


def _silu(v):
    return v * (1.0 / (1.0 + jnp.exp(-v)))


def _dot(a, b):
    return jnp.dot(a, b, preferred_element_type=F32)


def _dot_nt(a, b):
    return lax.dot_general(a, b, (((1,), (1,)), ((), ())), preferred_element_type=F32)


def _shift_rows(g, k, prev8):
    c = g.shape[1]
    r = pltpu.roll(g, k, 0)
    hr = pltpu.roll(prev8, k, 0)
    rows = lax.broadcasted_iota(jnp.int32, (SUBLANE, c), 0)
    top = jnp.where(rows < k, hr, r[0:SUBLANE])
    if g.shape[0] == SUBLANE:
        return top
    return jnp.concatenate([top, r[SUBLANE:]], axis=0)


def _causal_conv(g, w_ref, b_ref, prev8):
    width = w_ref.shape[0]
    y = g * w_ref[width - 1:width, :] + b_ref[...]
    for k in range(1, width):
        y = y + _shift_rows(g, k, prev8) * w_ref[width - 1 - k:width - k, :]
    return y


def _mods_kernel(c_ref, w_ref, b_ref, o_ref):
    h = _silu(c_ref[...]).astype(BF16)
    o_ref[...] = _dot(h, w_ref[...].astype(BF16)) + b_ref[...]


def _mods(c, ada_w, ada_b):
    depth, d, n = ada_w.shape
    b = c.shape[0]
    tn = _pick(n, 1024)
    return pl.pallas_call(
        _mods_kernel,
        grid=(depth, n // tn),
        in_specs=[pl.BlockSpec((b, d), lambda l, j: (0, 0)),
                  pl.BlockSpec((None, d, tn), lambda l, j: (l, 0, j)),
                  pl.BlockSpec((None, 1, tn), lambda l, j: (l, 0, j))],
        out_specs=pl.BlockSpec((None, b, tn), lambda l, j: (l, 0, j)),
        out_shape=jax.ShapeDtypeStruct((depth, b, n), F32),
        compiler_params=_params(2, 2 * d * tn * 4 + d * tn * 2),
        name="adaln_mods",
    )(c, ada_w, ada_b.reshape(depth, 1, n))


def _norm_mod_kernel(x_ref, g_ref, sh_ref, sc_ref, o_ref):
    rc = NORM_ROW_CHUNK
    gs = g_ref[...] * (1.0 + sc_ref[...])
    sh = sh_ref[...]

    def body(r, carry):
        rows = pl.ds(pl.multiple_of(r * rc, rc), rc)
        xf = x_ref[rows, :]
        inv = lax.rsqrt(jnp.mean(xf * xf, axis=-1, keepdims=True) + EPS)
        o_ref[rows, :] = ((xf * inv) * gs + sh).astype(o_ref.dtype)
        return carry

    lax.fori_loop(0, x_ref.shape[0] // rc, body, 0, unroll=4)


def _norm_mod(x, g, shift, scale):
    b, s, d = x.shape
    tm = _pick(s, TM_NORM)
    return pl.pallas_call(
        _norm_mod_kernel,
        grid=(b, s // tm),
        in_specs=[pl.BlockSpec((None, tm, d), lambda bi, i: (bi, i, 0)),
                  pl.BlockSpec((1, d), lambda bi, i: (0, 0)),
                  pl.BlockSpec((None, 1, d), lambda bi, i: (bi, 0, 0)),
                  pl.BlockSpec((None, 1, d), lambda bi, i: (bi, 0, 0))],
        out_specs=pl.BlockSpec((None, tm, d), lambda bi, i: (bi, i, 0)),
        out_shape=jax.ShapeDtypeStruct((b, s, d), BF16),
        compiler_params=_params(2, 8 * tm * d * 4),
        name="norm_mod",
    )(x, g.reshape(1, d), shift, scale)


def _matmul_kernel(h_ref, w_ref, o_ref):
    o_ref[...] = _dot(h_ref[...], w_ref[...].astype(BF16)).astype(o_ref.dtype)


def _w_spec(w, rows, cols, index):
    if w.ndim == 2:
        return pl.BlockSpec((rows, cols), lambda bi, i, j: index(bi, i, j)[1:])
    return pl.BlockSpec((None, rows, cols), index)


def _matmul(h, w, layer, col0, n, out_dtype, name, tn=None):
    b, s, k = h.shape
    tm = _pick(s, TM_MATMUL if w.dtype == BF16 else TM_PROJ)
    if tn is None:
        tn = _pick(n, TN_PROJ) if n % TN_PROJ == 0 else n
    assert col0 % tn == 0
    off = col0 // tn
    osz = jnp.dtype(out_dtype).itemsize
    vmem = 2 * tm * k * 2 + 2 * k * tn * w.dtype.itemsize + k * tn * 2 + 2 * tm * tn * osz + tm * tn * 4
    return pl.pallas_call(
        _matmul_kernel,
        grid=(b, s // tm, n // tn),
        in_specs=[pl.BlockSpec((None, tm, k), lambda bi, i, j: (bi, i, 0)),
                  _w_spec(w, k, tn, lambda bi, i, j: (layer, 0, j + off))],
        out_specs=pl.BlockSpec((None, tm, tn), lambda bi, i, j: (bi, i, j)),
        out_shape=jax.ShapeDtypeStruct((b, s, n), out_dtype),
        compiler_params=_params(3, vmem),
        name=name,
    )(h, w)


def _ffn_up_kernel(h_ref, wu_ref, wg_ref, cw_ref, cb_ref, o_ref, halo_scr):
    si = pl.program_id(1)
    j = pl.program_id(2)
    tm, th = o_ref.shape
    rc = _pick(tm, FFN_ROW_CHUNK)
    cc = _pick(th, FFN_COL_CHUNK)
    for c in range(th // cc):
        cols = slice(c * cc, (c + 1) * cc)
        prev8 = jnp.where(si == 0, 0.0, halo_scr[j, :, cols])
        wu = wu_ref[:, cols].astype(BF16)
        wg = wg_ref[:, cols].astype(BF16)
        for r in range(tm // rc):
            rows = slice(r * rc, (r + 1) * rc)
            hr = h_ref[rows, :]
            u = _dot(hr, wu)
            gate = _dot(hr, wg)
            gc = _causal_conv(gate, cw_ref.at[:, cols], cb_ref.at[:, cols], prev8)
            o_ref[rows, cols] = (_silu(gc) * u).astype(o_ref.dtype)
            prev8 = gate[rc - SUBLANE:rc, :]
        halo_scr[j, :, cols] = prev8


def _ffn_up(h, w_up, layer, conv_w, conv_b):
    b, s, d = h.shape
    hid = conv_w.shape[1]
    tm = _pick(s, TM_MATMUL)
    th = _pick(hid, TH_FFN)
    nj = hid // th
    vmem = 2 * tm * d * 2 + 4 * d * th * w_up.dtype.itemsize + 2 * d * th * 2 + 2 * tm * th * 2 + 4 * tm * th * 4
    return pl.pallas_call(
        _ffn_up_kernel,
        grid=(b, s // tm, nj),
        in_specs=[pl.BlockSpec((None, tm, d), lambda bi, i, j: (bi, i, 0)),
                  _w_spec(w_up, d, th, lambda bi, i, j: (layer, 0, j)),
                  _w_spec(w_up, d, th, lambda bi, i, j: (layer, 0, j + nj)),
                  pl.BlockSpec((FFN_CONV, th), lambda bi, i, j: (0, j)),
                  pl.BlockSpec((1, th), lambda bi, i, j: (0, j))],
        out_specs=pl.BlockSpec((None, tm, th), lambda bi, i, j: (bi, i, j)),
        out_shape=jax.ShapeDtypeStruct((b, s, hid), BF16),
        scratch_shapes=[pltpu.VMEM((nj, SUBLANE, th), F32)],
        compiler_params=_params(3, vmem),
        name="ffn_up",
    )(h, w_up, w_up, conv_w, conv_b.reshape(1, hid))


def _matmul_res_kernel(*refs, n_a):
    a_refs = refs[:n_a]
    w_refs = refs[n_a:2 * n_a]
    x_ref, gt_ref, o_ref = refs[2 * n_a:]
    acc = _dot(a_refs[0][...], w_refs[0][...].astype(BF16))
    for a_ref, w_ref in zip(a_refs[1:], w_refs[1:]):
        acc = acc + _dot(a_ref[...], w_ref[...].astype(BF16))
    o_ref[...] = x_ref[...] + gt_ref[...] * acc


def _matmul_res(a_list, w, layer, x, gate, name):
    b, s, n = x.shape
    ks = [a.shape[2] for a in a_list]
    tm = _pick(s, TM_PROJ)
    tn = _pick(n, TN_RES)
    in_specs = [pl.BlockSpec((None, tm, k), lambda bi, i, j: (bi, i, 0)) for k in ks]
    assert len(set(ks)) == 1
    in_specs += [_w_spec(w, ks[0], tn, functools.partial(lambda bi, i, j, p: (layer, p, j), p=p))
                 for p in range(len(ks))]
    in_specs += [pl.BlockSpec((None, tm, tn), lambda bi, i, j: (bi, i, j)),
                 pl.BlockSpec((None, 1, tn), lambda bi, i, j: (bi, 0, j))]
    ktot = sum(ks)
    vmem = 2 * tm * ktot * 2 + 2 * ktot * tn * w.dtype.itemsize + ktot * tn * 2 + 5 * tm * tn * 4
    return pl.pallas_call(
        functools.partial(_matmul_res_kernel, n_a=len(a_list)),
        grid=(b, s // tm, n // tn),
        in_specs=in_specs,
        out_specs=pl.BlockSpec((None, tm, tn), lambda bi, i, j: (bi, i, j)),
        out_shape=jax.ShapeDtypeStruct((b, s, n), F32),
        compiler_params=_params(3, vmem),
        name=name,
    )(*a_list, *([w] * len(a_list)), x, gate)


def _ret_log_gamma(h):
    return math.log1p(-(2.0 ** (-5.0 - h)))


def _retention_kernel(q_ref, k_ref, v_ref, g_ref, cos_ref, sin_ref, gn_ref, o_ref, r_scr):
    ci = pl.program_id(1)
    L = q_ref.shape[0]

    @pl.when(ci == 0)
    def _():
        r_scr[...] = jnp.zeros_like(r_scr)

    cos = cos_ref[...]
    sin = sin_ref[...]
    li = lax.broadcasted_iota(jnp.int32, (L, L), 0)
    si = lax.broadcasted_iota(jnp.int32, (L, L), 1)
    rel = (li - si).astype(F32)
    causal = li >= si
    pos = lax.broadcasted_iota(jnp.int32, (L, 1), 0).astype(F32)
    half = RET_DK // 2

    def rope(t):
        t1 = t[:, :half]
        t2 = t[:, half:]
        return jnp.concatenate([t1 * cos - t2 * sin, t1 * sin + t2 * cos], axis=1)

    for h in range(RET_HEADS):
        lg = _ret_log_gamma(h)
        q = rope(q_ref[:, h * RET_DK:(h + 1) * RET_DK])
        k = rope(k_ref[:, h * RET_DK:(h + 1) * RET_DK]) * (RET_DK ** -0.5)
        v = v_ref[:, h * RET_DV:(h + 1) * RET_DV].astype(F32)
        qb = q.astype(BF16)
        decay = jnp.where(causal, jnp.exp(jnp.where(causal, rel, 0.0) * lg), 0.0)
        inner = _dot_nt(qb, k.astype(BF16)) * decay
        xi = jnp.exp(lg * (pos + 1.0))
        zeta = jnp.exp(lg * (L - 1.0 - pos))
        r_old = r_scr[h]
        y = _dot(inner.astype(BF16), v.astype(BF16)) + _dot(qb, r_old.astype(BF16)) * xi
        kt = jnp.transpose(k).astype(BF16)
        r_scr[h] = math.exp(lg * L) * r_old + _dot(kt, (v * zeta).astype(BF16))
        yc = y - jnp.mean(y, axis=-1, keepdims=True)
        yn = yc * lax.rsqrt(jnp.mean(yc * yc, axis=-1, keepdims=True) + EPS)
        cols = slice(h * RET_DV, (h + 1) * RET_DV)
        gate = _silu(g_ref[:, cols].astype(F32))
        o_ref[:, cols] = (gate * (yn * gn_ref[:, cols])).astype(o_ref.dtype)


def _retention(qk, vg, cos, sin, gn_g):
    b, s, _ = qk.shape
    L = _pick(s, L_RET)
    vmem = 2 * (2 * L * RET_QK_W * 4 + 2 * L * RET_V_W * 2 + L * RET_V_W * 2) \
        + RET_HEADS * RET_DK * RET_DV * 4 + 24 * L * RET_DV * 4
    return pl.pallas_call(
        _retention_kernel,
        grid=(b, s // L),
        in_specs=[pl.BlockSpec((None, L, RET_QK_W), lambda bi, c: (bi, c, 0)),
                  pl.BlockSpec((None, L, RET_QK_W), lambda bi, c: (bi, c, 1)),
                  pl.BlockSpec((None, L, RET_V_W), lambda bi, c: (bi, c, 0)),
                  pl.BlockSpec((None, L, RET_V_W), lambda bi, c: (bi, c, 1)),
                  pl.BlockSpec((L, RET_DK // 2), lambda bi, c: (c, 0)),
                  pl.BlockSpec((L, RET_DK // 2), lambda bi, c: (c, 0)),
                  pl.BlockSpec((1, RET_V_W), lambda bi, c: (0, 0))],
        out_specs=pl.BlockSpec((None, L, RET_V_W), lambda bi, c: (bi, c, 0)),
        out_shape=jax.ShapeDtypeStruct((b, s, RET_V_W), BF16),
        scratch_shapes=[pltpu.VMEM((RET_HEADS, RET_DK, RET_DV), F32)],
        compiler_params=_params(2, vmem),
        name="retention",
    )(qk, qk, vg, vg, cos, sin, gn_g.reshape(1, RET_V_W))


def _rms_rows(x, g):
    return x * lax.rsqrt(jnp.mean(x * x, axis=-1, keepdims=True) + EPS) * g


def _rope_slot(t, cos_t, sin_t):
    lane = lax.broadcasted_iota(jnp.int32, t.shape, 1)
    hr = MLA_ROPE // 2
    swapped = jnp.where(lane < hr, pltpu.roll(t, LANE - hr, 1), pltpu.roll(t, hr, 1))
    return t * cos_t + swapped * sin_t


def _q_up_kernel(c_ref, g_ref, w_ref, cos_ref, sin_ref, o_ref):
    h = _rms_rows(c_ref[...], g_ref[...]).astype(BF16)
    q = _dot(h, w_ref[...]) * MLA_Q_SCALE
    cos_t = cos_ref[...]
    sin_t = sin_ref[...]
    for hd in range(MLA_HEADS):
        base = hd * MLA_QK_PAD
        o_ref[:, base:base + MLA_NOPE] = q[:, base:base + MLA_NOPE].astype(o_ref.dtype)
        rs = q[:, base + MLA_NOPE:base + MLA_QK_PAD]
        o_ref[:, base + MLA_NOPE:base + MLA_QK_PAD] = _rope_slot(rs, cos_t, sin_t).astype(o_ref.dtype)


def _kv_up_kernel(c_ref, kr_ref, g_ref, wk_ref, wv_ref, cos_ref, sin_ref, k_ref, v_ref):
    h = _rms_rows(c_ref[...], g_ref[...]).astype(BF16)
    kn = _dot(h, wk_ref[...])
    v_ref[...] = _dot(h, wv_ref[...]).astype(v_ref.dtype)
    kr = _rope_slot(kr_ref[...], cos_ref[...], sin_ref[...]).astype(k_ref.dtype)
    for hd in range(MLA_HEADS):
        base = hd * MLA_QK_PAD
        k_ref[:, base:base + MLA_NOPE] = kn[:, hd * MLA_NOPE:(hd + 1) * MLA_NOPE].astype(k_ref.dtype)
        k_ref[:, base + MLA_NOPE:base + MLA_QK_PAD] = kr


def _mla_up(small, q_g, kv_g, wq, wk, wv, cos_t, sin_t):
    b, s, _ = small.shape
    tm = _pick(s, TM_UP)
    nq = MLA_HEADS * MLA_QK_PAD
    nv = MLA_HEADS * MLA_V
    tab = pl.BlockSpec((tm, LANE), lambda bi, i: (i, 0))
    q = pl.pallas_call(
        _q_up_kernel,
        grid=(b, s // tm),
        in_specs=[pl.BlockSpec((None, tm, MLA_RANK), lambda bi, i: (bi, i, 0)),
                  pl.BlockSpec((1, MLA_RANK), lambda bi, i: (0, 0)),
                  pl.BlockSpec((MLA_RANK, nq), lambda bi, i: (0, 0)),
                  tab, tab],
        out_specs=pl.BlockSpec((None, tm, nq), lambda bi, i: (bi, i, 0)),
        out_shape=jax.ShapeDtypeStruct((b, s, nq), BF16),
        compiler_params=_params(2, 2 * MLA_RANK * nq * 2 + 2 * tm * nq * 2 + 3 * tm * nq * 4),
        name="mla_q_up",
    )(small, q_g.reshape(1, MLA_RANK), wq, cos_t, sin_t)
    k, v = pl.pallas_call(
        _kv_up_kernel,
        grid=(b, s // tm),
        in_specs=[pl.BlockSpec((None, tm, MLA_RANK), lambda bi, i: (bi, i, 1)),
                  pl.BlockSpec((None, tm, LANE), lambda bi, i: (bi, i, 2 * MLA_RANK // LANE)),
                  pl.BlockSpec((1, MLA_RANK), lambda bi, i: (0, 0)),
                  pl.BlockSpec((MLA_RANK, nv), lambda bi, i: (0, 0)),
                  pl.BlockSpec((MLA_RANK, nv), lambda bi, i: (0, 0)),
                  tab, tab],
        out_specs=[pl.BlockSpec((None, tm, nq), lambda bi, i: (bi, i, 0)),
                   pl.BlockSpec((None, tm, nv), lambda bi, i: (bi, i, 0))],
        out_shape=[jax.ShapeDtypeStruct((b, s, nq), BF16),
                   jax.ShapeDtypeStruct((b, s, nv), BF16)],
        compiler_params=_params(2, 4 * MLA_RANK * nv * 2 + 2 * tm * (nq + nv) * 2 + 3 * tm * nq * 4),
        name="mla_kv_up",
    )(small, small, kv_g.reshape(1, MLA_RANK), wk, wv, cos_t, sin_t)
    return q, k, v


MLA_Q_SCALE = ((MLA_NOPE + MLA_ROPE) ** -0.5) * math.log2(math.e)


def _mla_attn_kernel(q_ref, k_ref, v_ref, o_ref, *, tk):
    qi = pl.program_id(2)
    tq = q_ref.shape[0]
    q = q_ref[...]

    def update(carry, s2, v):
        m, l, acc = carry
        m_new = jnp.maximum(m, jnp.max(s2, axis=-1, keepdims=True))
        p = jnp.exp2(s2 - m_new)
        alpha = jnp.exp2(m - m_new)
        l = alpha * l + jnp.sum(p, axis=-1, keepdims=True)
        acc = alpha * acc + _dot(p.astype(BF16), v)
        return m_new, l, acc

    def body(ki, carry):
        rows = pl.ds(pl.multiple_of(ki * tk, tk), tk)
        return update(carry, _dot_nt(q, k_ref[rows, :]), v_ref[rows, :])

    init = (jnp.full((tq, 1), NEG, F32), jnp.zeros((tq, 1), F32), jnp.zeros((tq, MLA_V), F32))
    carry = lax.fori_loop(0, qi * (tq // tk), body, init)

    qc = lax.broadcasted_iota(jnp.int32, (tq, tk), 0) // MASK_CHUNK
    kc = lax.broadcasted_iota(jnp.int32, (tq, tk), 1) // MASK_CHUNK
    for j in range(tq // tk):
        rows = pl.ds(pl.multiple_of(qi * tq + j * tk, tk), tk)
        s2 = jnp.where(kc + (j * tk) // MASK_CHUNK <= qc, _dot_nt(q, k_ref[rows, :]), NEG)
        carry = update(carry, s2, v_ref[rows, :])
    m, l, acc = carry
    o_ref[...] = (acc / l).astype(o_ref.dtype)


def _mla_attention(q, k, v):
    b, s, _ = q.shape
    tq = _pick(s, TQ_ATTN)
    tk = _pick(tq, TK_ATTN)
    vmem = 2 * (tq * MLA_QK_PAD * 2 + s * MLA_QK_PAD * 2 + s * MLA_V * 2 + tq * MLA_V * 2) \
        + 6 * tq * tk * 4
    return pl.pallas_call(
        functools.partial(_mla_attn_kernel, tk=tk),
        grid=(b, MLA_HEADS, s // tq),
        in_specs=[pl.BlockSpec((None, tq, MLA_QK_PAD), lambda bi, h, i: (bi, i, h)),
                  pl.BlockSpec((None, s, MLA_QK_PAD), lambda bi, h, i: (bi, 0, h)),
                  pl.BlockSpec((None, s, MLA_V), lambda bi, h, i: (bi, 0, h))],
        out_specs=pl.BlockSpec((None, tq, MLA_V), lambda bi, h, i: (bi, i, h)),
        out_shape=jax.ShapeDtypeStruct((b, s, MLA_HEADS * MLA_V), BF16),
        compiler_params=_params(3, vmem),
        name="mla_attention",
    )(q, k, v)


def _proj_conv_kernel(h_ref, w_ref, cw_ref, cb_ref, o_ref, halo_scr):
    si = pl.program_id(1)
    j = pl.program_id(2)
    tm, tn = o_ref.shape
    rc = _pick(tm, FFN_ROW_CHUNK)
    cc = _pick(tn, FFN_COL_CHUNK)
    for c in range(tn // cc):
        cols = slice(c * cc, (c + 1) * cc)
        prev8 = jnp.where(si == 0, 0.0, halo_scr[j, :, cols])
        wc = w_ref[:, cols].astype(BF16)
        for r in range(tm // rc):
            rows = slice(r * rc, (r + 1) * rc)
            raw = _dot(h_ref[rows, :], wc)
            o_ref[rows, cols] = _silu(_causal_conv(raw, cw_ref.at[:, cols], cb_ref.at[:, cols], prev8))
            prev8 = raw[rc - SUBLANE:rc, :]
        halo_scr[j, :, cols] = prev8


def _proj_conv(h, w, layer, col0, conv_w, conv_b):
    b, s, k = h.shape
    n = conv_w.shape[1]
    tm = _pick(s, TM_PROJ)
    tn = _pick(n, TN_PROJ)
    nj = n // tn
    assert col0 % tn == 0
    off = col0 // tn
    vmem = 2 * tm * k * 2 + 2 * k * tn * w.dtype.itemsize + k * tn * 2 + 2 * tm * tn * 4 + 4 * tm * tn * 4
    return pl.pallas_call(
        _proj_conv_kernel,
        grid=(b, s // tm, nj),
        in_specs=[pl.BlockSpec((None, tm, k), lambda bi, i, j: (bi, i, 0)),
                  _w_spec(w, k, tn, lambda bi, i, j: (layer, 0, j + off)),
                  pl.BlockSpec((SSD_CONV, tn), lambda bi, i, j: (0, j)),
                  pl.BlockSpec((1, tn), lambda bi, i, j: (0, j))],
        out_specs=pl.BlockSpec((None, tm, tn), lambda bi, i, j: (bi, i, j)),
        out_shape=jax.ShapeDtypeStruct((b, s, n), F32),
        scratch_shapes=[pltpu.VMEM((nj, SUBLANE, tn), F32)],
        compiler_params=_params(3, vmem),
        name="ssd_in_xbc_conv",
    )(h, w, conv_w, conv_b.reshape(1, n))


def _softplus_kernel(x_ref, b_ref, o_ref):
    v = x_ref[...] + b_ref[...]
    o_ref[...] = jnp.maximum(v, 0.0) + jnp.log1p(jnp.exp(-jnp.abs(v)))


def _ssd_dt(dt_raw, dt_bias_pad):
    b, s, n = dt_raw.shape
    tm = _pick(s, 1024)
    return pl.pallas_call(
        _softplus_kernel,
        grid=(b, s // tm),
        in_specs=[pl.BlockSpec((None, tm, n), lambda bi, i: (bi, i, 0)),
                  pl.BlockSpec((1, n), lambda bi, i: (0, 0))],
        out_specs=pl.BlockSpec((None, tm, n), lambda bi, i: (bi, i, 0)),
        out_shape=jax.ShapeDtypeStruct((b, s, n), F32),
        compiler_params=_params(2, 8 * tm * n * 4),
        name="ssd_dt",
    )(dt_raw, dt_bias_pad.reshape(1, n))


def _split3(a):
    hi = a.astype(BF16)
    r1 = a - hi.astype(F32)
    mid = r1.astype(BF16)
    lo = (r1 - mid.astype(F32)).astype(BF16)
    return hi, mid, lo


SSD_PIECES = 3
SSD_QW = SSD_PIECES * SSD_HPG
SSD_NCOPY = 3 * SSD_PIECES


def _ssd_expand_matrix():
    r = np.arange(LANE)[:, None]
    c = np.arange(3 * SSD_GW)[None, :]
    hit = (r < 3 * SSD_QW) & (c // SSD_GW == r // SSD_QW) & ((c % SSD_GW) // SSD_HEADDIM == r % SSD_HPG)
    return jnp.asarray(hit, dtype=BF16)


def _ssd_scan_kernel(x_ref, b_ref, c_ref, dtc_ref, dtr_ref, ac_ref, ar_ref, dsk_ref, ex_ref,
                     o_ref, st_scr):
    ci = pl.program_id(2)
    L = x_ref.shape[0]

    @pl.when(ci == 0)
    def _():
        st_scr[...] = jnp.zeros_like(st_scr)

    li = lax.broadcasted_iota(jnp.int32, (L, L), 0)
    si = lax.broadcasted_iota(jnp.int32, (L, L), 1)
    causal = li >= si
    tril = jnp.where(causal, 1.0, 0.0).astype(BF16)
    triu = jnp.where(li <= si, 1.0, 0.0).astype(BF16)
    lane = lax.broadcasted_iota(jnp.int32, (L, LANE), 1)
    piece = (lane % SSD_QW) // SSD_HPG
    left = lane < SSD_HEADDIM

    for gi in range(dtr_ref.shape[0]):
        gcols = slice(gi * SSD_GW, (gi + 1) * SSD_GW)
        ncols = slice(gi * SSD_STATE, (gi + 1) * SSD_STATE)
        lcols = slice(gi * LANE, (gi + 1) * LANE)

        dt_c = dtc_ref[:, lcols]
        a_c = -jnp.exp(ac_ref[:, lcols])
        acum = sum(_dot(tril, part) for part in _split3(dt_c * a_c))
        tot = acum[L - 1:L, :]
        quantity = jnp.where(lane < SSD_QW, dt_c,
                             jnp.where(lane < 2 * SSD_QW, jnp.exp(acum), jnp.exp(tot - acum)))
        hi, mid, lo = (p.astype(F32) for p in _split3(quantity))
        cols = jnp.where(piece == 0, hi, jnp.where(piece == 1, mid, lo)).astype(BF16)
        expanded = _dot(cols, ex_ref[...])
        dt_x = expanded[:, 0:SSD_GW]
        ea_x = expanded[:, SSD_GW:2 * SSD_GW]
        te_x = expanded[:, 2 * SSD_GW:3 * SSD_GW]

        acum_r = sum(_dot(part, triu) for part in _split3(dtr_ref[gi] * -jnp.exp(ar_ref[gi])))

        xs = x_ref[:, gcols]
        bm = b_ref[:, ncols]
        cm = c_ref[:, ncols].astype(BF16)
        cb = _dot_nt(cm, bm.astype(BF16))
        state = st_scr[gi]
        cs = _dot(cm, state.astype(BF16))
        bmt = jnp.transpose(bm).astype(BF16)
        xdt = xs * dt_x

        ys = []
        for t in range(SSD_GW // LANE):
            x2 = xdt[:, t * LANE:(t + 1) * LANE]
            y_t = None
            for h, keep in ((2 * t, left), (2 * t + 1, jnp.logical_not(left))):
                seg = acum[:, h:h + 1] - acum_r[h:h + 1, :]
                lmat = jnp.exp(jnp.where(causal, seg, NEG))
                part = _dot((cb * lmat).astype(BF16), jnp.where(keep, x2, 0.0).astype(BF16))
                y_t = part if y_t is None else y_t + part
            ys.append(y_t)
        y_diag = jnp.concatenate(ys, axis=1)
        o_ref[:, gcols] = y_diag + cs * ea_x + xs * dsk_ref[:, gcols]
        st_scr[gi] = state * ea_x[L - 1:L, :] + _dot(bmt, (xdt * te_x).astype(BF16))


def _ssd_scan(xbc, dtp, a_log, d_skip):
    b, s, _ = xbc.shape
    L = _pick(s, L_SSD)
    g = SSD_GROUPS
    gps = SSD_GROUPS_PER_STEP
    dt_g = dtp.reshape(b, s, g, SSD_HPG)
    def group_lanes(t):
        t = jnp.tile(t, LANE // SSD_HPG)
        return t.reshape(t.shape[:-2] + (g * LANE,))

    dtc = group_lanes(dt_g)
    dtr = jnp.transpose(dt_g, (0, 2, 3, 1))
    a_col = group_lanes(a_log.reshape(1, g, SSD_HPG))
    a_row = a_log.reshape(g, SSD_HPG, 1)
    dsk = jnp.repeat(d_skip, SSD_HEADDIM).reshape(1, SSD_INNER)
    xw = gps * SSD_GW
    nw = gps * SSD_STATE
    boff = SSD_INNER // nw
    coff = boff + g // gps
    vmem = gps * (4 * L * SSD_GW * 4 + 24 * L * L * 4 + 16 * L * SSD_GW * 4)
    return pl.pallas_call(
        _ssd_scan_kernel,
        grid=(b, g // gps, s // L),
        in_specs=[pl.BlockSpec((None, L, xw), lambda bi, gi, c: (bi, c, gi)),
                  pl.BlockSpec((None, L, nw), lambda bi, gi, c: (bi, c, boff + gi)),
                  pl.BlockSpec((None, L, nw), lambda bi, gi, c: (bi, c, coff + gi)),
                  pl.BlockSpec((None, L, gps * LANE), lambda bi, gi, c: (bi, c, gi)),
                  pl.BlockSpec((None, gps, SSD_HPG, L), lambda bi, gi, c: (bi, gi, 0, c)),
                  pl.BlockSpec((1, gps * LANE), lambda bi, gi, c: (0, gi)),
                  pl.BlockSpec((gps, SSD_HPG, 1), lambda bi, gi, c: (gi, 0, 0)),
                  pl.BlockSpec((1, xw), lambda bi, gi, c: (0, gi)),
                  pl.BlockSpec((LANE, 3 * SSD_GW), lambda bi, gi, c: (0, 0))],
        out_specs=pl.BlockSpec((None, L, xw), lambda bi, gi, c: (bi, c, gi)),
        out_shape=jax.ShapeDtypeStruct((b, s, SSD_INNER), F32),
        scratch_shapes=[pltpu.VMEM((gps, SSD_STATE, SSD_GW), F32)],
        compiler_params=_params(3, vmem),
        name="ssd_scan",
    )(xbc, xbc, xbc, dtc, dtr, a_col, a_row, dsk, _ssd_expand_matrix())


def _gate_norm_kernel(y_ref, z_ref, g_ref, o_ref):
    v = y_ref[...] * _silu(z_ref[...].astype(F32))
    o_ref[...] = _rms_rows(v, g_ref[...]).astype(o_ref.dtype)


def _gate_norm(y, z, g):
    b, s, n = y.shape
    tm = _pick(s, 256)
    return pl.pallas_call(
        _gate_norm_kernel,
        grid=(b, s // tm),
        in_specs=[pl.BlockSpec((None, tm, n), lambda bi, i: (bi, i, 0)),
                  pl.BlockSpec((None, tm, n), lambda bi, i: (bi, i, 0)),
                  pl.BlockSpec((1, n), lambda bi, i: (0, 0))],
        out_specs=pl.BlockSpec((None, tm, n), lambda bi, i: (bi, i, 0)),
        out_shape=jax.ShapeDtypeStruct((b, s, n), BF16),
        compiler_params=_params(2, 8 * tm * n * 4),
        name="ssd_gate_norm",
    )(y, z, g.reshape(1, n))


def _final_norm_kernel(x_ref, g_ref, o_ref):
    o_ref[...] = _rms_rows(x_ref[...], g_ref[...])


def _final_norm(x, g):
    b, s, n = x.shape
    tm = _pick(s, 512)
    return pl.pallas_call(
        _final_norm_kernel,
        grid=(b, s // tm),
        in_specs=[pl.BlockSpec((None, tm, n), lambda bi, i: (bi, i, 0)),
                  pl.BlockSpec((1, n), lambda bi, i: (0, 0))],
        out_specs=pl.BlockSpec((None, tm, n), lambda bi, i: (bi, i, 0)),
        out_shape=jax.ShapeDtypeStruct((b, s, n), F32),
        compiler_params=_params(2, 8 * tm * n * 4),
        name="final_norm",
    )(x, g.reshape(1, n))


def _rope_tables(s):
    pos = jnp.arange(s, dtype=jnp.int32).astype(F32)[:, None]
    half = RET_DK // 2
    inv = RET_THETA ** (-jnp.arange(half, dtype=F32) / half)
    ang = pos * inv[None, :]
    ret_cos, ret_sin = jnp.cos(ang), jnp.sin(ang)
    hr = MLA_ROPE // 2
    inv_m = MLA_THETA ** (-jnp.arange(hr, dtype=F32) / hr)
    ang_m = pos * inv_m[None, :]
    cm, sm = jnp.cos(ang_m), jnp.sin(ang_m)
    pad = LANE - MLA_ROPE
    mla_cos = jnp.concatenate([cm, cm, jnp.ones((s, pad), F32)], axis=1)
    mla_sin = jnp.concatenate([-sm, sm, jnp.zeros((s, pad), F32)], axis=1)
    return ret_cos, ret_sin, mla_cos, mla_sin


def _hybrid_layer(x, mods, norm_g, w_in, layer, q_g, w_uq, kv_g, w_ukv, gn_g, w_out, tables):
    sh, sc, gt = mods
    ret_cos, ret_sin, mla_cos, mla_sin = tables
    o_v = 2 * RET_QK_W
    o_c = o_v + 2 * RET_V_W
    o_kr = o_c + 2 * MLA_RANK
    h = _norm_mod(x, norm_g, sh, sc)
    qk = _matmul(h, w_in, layer, 0, o_v, F32, "hyb_in_qk")
    vg = _matmul(h, w_in, layer, o_v, o_c - o_v, BF16, "hyb_in_vg")
    w_small = jnp.concatenate(
        [w_in[layer, :, o_c:o_kr + MLA_ROPE], jnp.zeros((D_MODEL, LANE - MLA_ROPE), BF16)], axis=1)
    small = _matmul(h, w_small, None, 0, w_small.shape[1], F32, "hyb_in_latent")
    y_ret = _retention(qk, vg, ret_cos, ret_sin, gn_g)

    wq = w_uq.astype(BF16).reshape(MLA_RANK, MLA_HEADS, MLA_NOPE + MLA_ROPE)
    wq = jnp.pad(wq, ((0, 0), (0, 0), (0, MLA_QK_PAD - MLA_NOPE - MLA_ROPE)))
    wq = wq.reshape(MLA_RANK, MLA_HEADS * MLA_QK_PAD)
    wkv = w_ukv.astype(BF16).reshape(MLA_RANK, MLA_HEADS, MLA_NOPE + MLA_V)
    wk = wkv[:, :, :MLA_NOPE].reshape(MLA_RANK, MLA_HEADS * MLA_NOPE)
    wv = wkv[:, :, MLA_NOPE:].reshape(MLA_RANK, MLA_HEADS * MLA_V)
    q, k, v = _mla_up(small, q_g, kv_g, wq, wk, wv, mla_cos, mla_sin)
    y_mla = _mla_attention(q, k, v)
    return _matmul_res([y_ret, y_mla], w_out, layer, x, gt, "hyb_out")


def _ssd_layer(x, mods, norm_g, w_in, layer, conv_w, conv_b, dt_bias, a_log, d_skip, ssd_g, w_out):
    sh, sc, gt = mods
    o_x = SSD_INNER
    o_dt = o_x + SSD_CONV_DIM
    h = _norm_mod(x, norm_g, sh, sc)
    z = _matmul(h, w_in, layer, 0, o_x, BF16, "ssd_in_z")
    xbc = _proj_conv(h, w_in, layer, o_x, conv_w, conv_b)
    dt_raw = _matmul(h, w_in, layer, o_dt, LANE, F32, "ssd_in_dt")
    dtp = _ssd_dt(dt_raw, jnp.pad(dt_bias, (0, LANE - SSD_HEADS)))[:, :, :SSD_HEADS]
    y = _ssd_scan(xbc, dtp, a_log, d_skip)
    hn = _gate_norm(y, z, ssd_g)
    return _matmul_res([hn], w_out, layer, x, gt, "ssd_out")


def _ffn_layer(x, mods, norm_g, w_up, w_down, layer, conv_w, conv_b):
    sh, sc, gt = mods
    a = _ffn_up(_norm_mod(x, norm_g, sh, sc), w_up, layer, conv_w, conv_b)
    return _matmul_res([a], w_down, layer, x, gt, "ffn_down")


def kernel(x, c, ada_w, ada_b, norm_mix_g, norm_ffn_g, hyb_w_in, hyb_q_norm_g, hyb_w_uq, hyb_kv_norm_g, hyb_w_ukv, hyb_ret_gn_g, hyb_w_out, ssd_w_in, ssd_conv_w, ssd_conv_b, ssd_dt_bias, ssd_a_log, ssd_d, ssd_norm_g, ssd_w_out, ffn_w_up, ffn_conv_w, ffn_conv_b, ffn_w_down, final_norm_g):
    b, s, d = x.shape
    depth = ada_w.shape[0]
    mods = _mods(c, ada_w, ada_b).reshape(depth, b, 6, 1, d)
    tables = _rope_tables(s)
    hyb_w_in, hyb_w_out, ssd_w_in, ssd_w_out, ffn_w_down = (
        w.astype(BF16) for w in (hyb_w_in, hyb_w_out, ssd_w_in, ssd_w_out, ffn_w_down))
    for l in range(depth):
        m = [mods[l, :, k] for k in range(6)]
        i = l // 2
        if l % 2 == 0:
            x = _hybrid_layer(x, m[0:3], norm_mix_g[l], hyb_w_in, i, hyb_q_norm_g[i], hyb_w_uq[i],
                              hyb_kv_norm_g[i], hyb_w_ukv[i], hyb_ret_gn_g[i], hyb_w_out, tables)
        else:
            x = _ssd_layer(x, m[0:3], norm_mix_g[l], ssd_w_in, i, ssd_conv_w[i], ssd_conv_b[i],
                           ssd_dt_bias[i], ssd_a_log[i], ssd_d[i], ssd_norm_g[i], ssd_w_out)
        x = _ffn_layer(x, m[3:6], norm_ffn_g[l], ffn_w_up, ffn_w_down, l, ffn_conv_w[l], ffn_conv_b[l])
    return _final_norm(x, final_norm_g)
```
